```python
import jax, jax.numpy as jnp
from jax import lax
import numpy as np

D_MODEL = 1024
BATCH = 8
SEQ = 4096
DEPTH = 1

CHUNK = 64
Q_BLOCK = 128

HEAD_DIM = 64
N_SB_HEADS = 8
D_SB = N_SB_HEADS * HEAD_DIM
N_MLA_HEADS = 8
QK_NOPE_DIM = 64
QK_ROPE_DIM = 32
V_HEAD_DIM = 64
Q_LORA_RANK = 256
KV_LORA_RANK = 128
D_MLA = N_MLA_HEADS * V_HEAD_DIM
D_MIX = D_SB + D_MLA
ROPE_THETA = 10000.0
PLE_DIM = 256
EPS = 1e-6

IN_SPLITS = (D_SB, D_SB, D_SB, D_SB, Q_LORA_RANK, KV_LORA_RANK, QK_ROPE_DIM, D_MLA)
D_IN = sum(IN_SPLITS)
IN_SPLIT_IDX = tuple(int(v) for v in np.cumsum(IN_SPLITS)[:-1])

kernel_name = "hybrid_stickbreak_mla_block"


def rms_norm(x, g):
    xf = x.astype(jnp.float32)
    y = xf * lax.rsqrt(jnp.mean(xf * xf, axis=-1, keepdims=True) + EPS)
    return (y * g.astype(jnp.float32)).astype(x.dtype)


def head_rms_norm(o, g):
    B, S, H, d = o.shape
    return rms_norm(o, g.reshape(H, d)).reshape(B, S, H * d)


def to_blocks(a):
    B, S = a.shape[:2]
    return a.reshape(B, S // Q_BLOCK, Q_BLOCK, *a.shape[2:]).swapaxes(0, 1)


def from_blocks(a):
    a = a.swapaxes(0, 1)
    return a.reshape(a.shape[0], a.shape[1] * a.shape[2], *a.shape[3:])


def apply_rope(x, positions):
    half = x.shape[-1] // 2
    freq = ROPE_THETA ** (-jnp.arange(half, dtype=jnp.float32) / half)
    ang = positions.astype(jnp.float32)[..., None] * freq
    ang = ang.reshape(ang.shape[:2] + (1,) * (x.ndim - 3) + (half,))
    cos, sin = jnp.cos(ang).astype(x.dtype), jnp.sin(ang).astype(x.dtype)
    x1, x2 = x[..., :half], x[..., half:]
    return jnp.concatenate([x1 * cos - x2 * sin, x2 * cos + x1 * sin], axis=-1)


def stick_breaking_attention(q, k, v):
    S = k.shape[1]
    scale = HEAD_DIM ** -0.5
    key_idx = jnp.arange(S)

    def block(args):
        b_idx, q_blk = args
        z = jnp.einsum('bqhd,bkhd->bhqk', q_blk, k).astype(jnp.float32) * scale
        t_idx = b_idx * Q_BLOCK + jnp.arange(Q_BLOCK)
        past = key_idx[None, :] < t_idx[:, None]
        log_fail = jnp.where(past, jax.nn.log_sigmoid(-z), 0.0)
        suffix = lax.cumsum(log_fail, axis=3, reverse=True) - log_fail
        w = jnp.where(past, jnp.exp(jax.nn.log_sigmoid(z) + suffix), 0.0)
        return jnp.einsum('bhqk,bkhd->bqhd', w.astype(v.dtype), v)

    out = lax.map(block, (jnp.arange(S // Q_BLOCK), to_blocks(q)))
    return from_blocks(out)


def latent_attention(q_nope, q_rope, k_nope, k_rope, v):
    S = k_nope.shape[1]
    scale = (QK_NOPE_DIM + QK_ROPE_DIM) ** -0.5
    key_chunk = jnp.arange(S) // CHUNK

    def block(args):
        b_idx, qn, qr = args
        z = (jnp.einsum('bqhd,bkhd->bhqk', qn, k_nope)
             + jnp.einsum('bqhr,bkr->bhqk', qr, k_rope)).astype(jnp.float32) * scale
        q_chunk = (b_idx * Q_BLOCK + jnp.arange(Q_BLOCK)) // CHUNK
        visible = key_chunk[None, :] <= q_chunk[:, None]
        z = jnp.where(visible, z, -jnp.inf)
        w = jax.nn.softmax(z, axis=-1)
        return jnp.einsum('bhqk,bkhd->bqhd', w.astype(v.dtype), v)

    out = lax.map(block, (jnp.arange(S // Q_BLOCK), to_blocks(q_nope), to_blocks(q_rope)))
    return from_blocks(out)


def setup_inputs(seed: int = 0) -> dict:
    key = jax.random.key(seed)
    ks = jax.random.split(key, 20)

    def w(k, shape, fan_in):
        return jax.random.normal(k, shape, jnp.float32) * fan_in ** -0.5

    def gain(k, n):
        return 1.0 + 0.05 * jax.random.normal(k, (DEPTH, n), jnp.float32)

    x = jax.random.normal(ks[0], (BATCH, SEQ, D_MODEL), jnp.float32)
    p = jax.random.normal(ks[1], (DEPTH, BATCH, SEQ, PLE_DIM), jnp.float32)
    start = jax.random.randint(ks[2], (BATCH, 1), 0, 4096, dtype=jnp.int32)
    positions = start + jnp.arange(SEQ, dtype=jnp.int32)[None, :]
    return {
        'x': x,
        'p': p,
        'positions': positions,
        'norm_pre_g': gain(ks[3], D_MODEL),
        'w_in': w(ks[4], (DEPTH, D_MODEL, D_IN), D_MODEL),
        'q_norm_g': gain(ks[5], Q_LORA_RANK),
        'w_uq': w(ks[6], (DEPTH, Q_LORA_RANK, N_MLA_HEADS * (QK_NOPE_DIM + QK_ROPE_DIM)), Q_LORA_RANK),
        'kv_norm_g': gain(ks[7], KV_LORA_RANK),
        'w_ukv': w(ks[8], (DEPTH, KV_LORA_RANK, N_MLA_HEADS * (QK_NOPE_DIM + V_HEAD_DIM)), KV_LORA_RANK),
        'sb_out_norm_g': gain(ks[9], D_SB),
        'mla_out_norm_g': gain(ks[10], D_MLA),
        'w_out': w(ks[11], (DEPTH, D_MIX, D_MODEL), D_MIX),
        'norm_post_g': gain(ks[12], D_MODEL),
        'w_ple': w(ks[13], (DEPTH, PLE_DIM, D_MODEL), PLE_DIM),
        'ple_norm_g': gain(ks[14], D_MODEL),
        'w_ple_gate': w(ks[15], (DEPTH, D_MODEL, D_MODEL), D_MODEL),
        'b_ple_gate': 0.02 * jax.random.normal(ks[16], (DEPTH, D_MODEL), jnp.float32),
    }


def reference(x, p, positions, norm_pre_g, w_in, q_norm_g, w_uq, kv_norm_g, w_ukv,
              sb_out_norm_g, mla_out_norm_g, w_out, norm_post_g, w_ple, ple_norm_g,
              w_ple_gate, b_ple_gate):
    B, S, _ = x.shape
    for i in range(DEPTH):
        h = rms_norm(x, norm_pre_g[i])
        proj = h @ w_in[i]
        sb_q, sb_k, sb_v, sb_g, c_q, c_kv, k_rope, mla_g = jnp.split(proj, IN_SPLIT_IDX, axis=-1)

        sb_o = stick_breaking_attention(sb_q.reshape(B, S, N_SB_HEADS, HEAD_DIM),
                                        sb_k.reshape(B, S, N_SB_HEADS, HEAD_DIM),
                                        sb_v.reshape(B, S, N_SB_HEADS, HEAD_DIM))
        sb_y = head_rms_norm(sb_o, sb_out_norm_g[i]) * jax.nn.silu(sb_g)

        q = (rms_norm(c_q, q_norm_g[i]) @ w_uq[i]).reshape(B, S, N_MLA_HEADS, QK_NOPE_DIM + QK_ROPE_DIM)
        q_nope, q_rope = q[..., :QK_NOPE_DIM], apply_rope(q[..., QK_NOPE_DIM:], positions)
        kv = (rms_norm(c_kv, kv_norm_g[i]) @ w_ukv[i]).reshape(B, S, N_MLA_HEADS, QK_NOPE_DIM + V_HEAD_DIM)
        k_nope, v = kv[..., :QK_NOPE_DIM], kv[..., QK_NOPE_DIM:]
        k_rope = apply_rope(k_rope, positions)
        mla_o = latent_attention(q_nope, q_rope, k_nope, k_rope, v)
        mla_y = head_rms_norm(mla_o, mla_out_norm_g[i]) * jax.nn.silu(mla_g)

        y = jnp.concatenate([sb_y, mla_y], axis=-1) @ w_out[i]
        x = x + rms_norm(y, norm_post_g[i])

        ple = rms_norm(p[i] @ w_ple[i], ple_norm_g[i])
        x = x + ple * jax.nn.sigmoid(x @ w_ple_gate[i] + b_ple_gate[i])
    return x
```

```python
import functools

import jax
import jax.numpy as jnp
import numpy as np
from jax import lax
from jax.experimental import pallas as pl
from jax.experimental.pallas import tpu as pltpu

HEAD_DIM = 64
N_HEADS = 8
N_PAIRS = N_HEADS // 2
LANES = 128
QK_NOPE_DIM = 64
QK_ROPE_DIM = 32
Q_LORA_RANK = 256
KV_LORA_RANK = 128
CHUNK = 64
ROPE_THETA = 10000.0
EPS = 1e-6
D_GROUP = N_HEADS * HEAD_DIM

PROJ_ROWS = 512
ATT_TILE = 256
VMEM_LIMIT = 56 * 1024 * 1024

F32 = jnp.float32
BF16 = jnp.bfloat16


def _rms(x, g):
    return x * lax.rsqrt(jnp.mean(x * x, axis=-1, keepdims=True) + EPS) * g


def _dot(a, b):
    return jnp.dot(a, b, preferred_element_type=F32)


def _dot_nt(a, b):
    return lax.dot_general(a, b, (((1,), (1,)), ((), ())), preferred_element_type=F32)


def _sigmoid(x):
    return 1.0 / (1.0 + jnp.exp(-x))


def _proj_kernel(x_ref, pos_ref, freq_ref, gpre_ref, wq_ref, wk_ref, wv_ref, wg_ref, wcq_ref, wckv_ref,
                 wkr_ref, wmg_ref, gq_ref, wuq_ref, gkv_ref, wukv_ref,
                 sbq_ref, sbk_ref, sbv_ref, sbg_ref, mq_ref, mk_ref, mv_ref, mg_ref):
    h = _rms(x_ref[...], gpre_ref[...]).astype(BF16)
    tm = h.shape[0]
    lane = lax.broadcasted_iota(jnp.int32, (tm, LANES), 1)
    low_half = lane < HEAD_DIM

    def split_pairs(a, out_ref):
        for p in range(N_PAIRS):
            ap = a[:, p * LANES:(p + 1) * LANES]
            out_ref[0, 2 * p] = jnp.where(low_half, ap, 0.0).astype(BF16)
            out_ref[0, 2 * p + 1] = jnp.where(low_half, 0.0, ap).astype(BF16)

    sbq_ref[0] = (_dot(h, wq_ref[...]) * (HEAD_DIM ** -0.5)).astype(BF16)
    split_pairs(_dot(h, wk_ref[...]), sbk_ref)
    split_pairs(_dot(h, wv_ref[...]), sbv_ref)
    sbg_ref[0] = _dot(h, wg_ref[...])
    mg_ref[0] = _dot(h, wmg_ref[...])

    ang = pos_ref[...] * freq_ref[...]
    cos, sin = jnp.cos(ang), jnp.sin(ang)

    cq = _rms(_dot(h, wcq_ref[...]), gq_ref[...]).astype(BF16)
    q2 = _dot(cq, wuq_ref[...])
    scale = (QK_NOPE_DIM + QK_ROPE_DIM) ** -0.5
    half = N_HEADS * LANES
    for hd in range(N_HEADS):
        qa = q2[:, hd * LANES:(hd + 1) * LANES]
        qb = q2[:, half + hd * LANES:half + (hd + 1) * LANES]
        mq_ref[0, :, hd * LANES:(hd + 1) * LANES] = ((qa * cos + qb * sin) * scale).astype(BF16)

    ckv = _rms(_dot(h, wckv_ref[...]), gkv_ref[...]).astype(BF16)
    kv = _dot(ckv, wukv_ref[...])
    kr2 = _dot(h, wkr_ref[...])
    kr = kr2[:, :LANES] * cos + kr2[:, LANES:] * sin
    for hd in range(N_HEADS):
        mk_ref[0, hd] = (kv[:, hd * LANES:(hd + 1) * LANES] + kr).astype(BF16)
    split_pairs(kv[:, half:], mv_ref)


def _projections(x, pos, freq, gpre, wq, wk, wv, wg, wcq, wckv, wkr, wmg, gq, wuq, gkv, wukv):
    B, S, D = x.shape
    tm = min(PROJ_ROWS, S)
    assert S % tm == 0
    nt = S // tm
    full = lambda a: pl.BlockSpec(a.shape, lambda b, t: (0,) * a.ndim)
    row = lambda w: pl.BlockSpec((1, tm, w), lambda b, t: (b, t, 0))
    head = pl.BlockSpec((1, N_HEADS, tm, LANES), lambda b, t: (b, 0, t, 0))
    head_shape = jax.ShapeDtypeStruct((B, N_HEADS, S, LANES), BF16)
    weights = (freq, gpre, wq, wk, wv, wg, wcq, wckv, wkr, wmg, gq, wuq, gkv, wukv)
    return pl.pallas_call(
        _proj_kernel,
        grid=(B, nt),
        in_specs=[pl.BlockSpec((None, tm, D), lambda b, t: (b, t, 0)),
                  pl.BlockSpec((None, tm, 1), lambda b, t: (b, t, 0))] + [full(w) for w in weights],
        out_specs=[row(D_GROUP), head, head, row(D_GROUP), row(N_HEADS * LANES), head, head, row(D_GROUP)],
        out_shape=[jax.ShapeDtypeStruct((B, S, D_GROUP), BF16), head_shape, head_shape,
                   jax.ShapeDtypeStruct((B, S, D_GROUP), F32),
                   jax.ShapeDtypeStruct((B, S, N_HEADS * LANES), BF16), head_shape, head_shape,
                   jax.ShapeDtypeStruct((B, S, D_GROUP), F32)],
        compiler_params=pltpu.CompilerParams(dimension_semantics=("arbitrary", "arbitrary"),
                                             vmem_limit_bytes=VMEM_LIMIT),
        name="proj_in",
    )(x, pos, *weights)


def _head_norm_gate(o_pair, g, gate):
    lane = lax.broadcasted_iota(jnp.int32, o_pair.shape, 1)
    low_half = lane < HEAD_DIM
    sq = o_pair * o_pair
    ms_lo = jnp.sum(jnp.where(low_half, sq, 0.0), axis=-1, keepdims=True) * (1.0 / HEAD_DIM)
    ms_hi = jnp.sum(jnp.where(low_half, 0.0, sq), axis=-1, keepdims=True) * (1.0 / HEAD_DIM)
    inv = jnp.where(low_half, lax.rsqrt(ms_lo + EPS), lax.rsqrt(ms_hi + EPS))
    return o_pair * inv * g * (gate * _sigmoid(gate))


def _sb_kernel(q_ref, k_ref, v_ref, gate_ref, g_ref, o_ref):
    tq = q_ref.shape[1]
    tk = tq
    i = pl.program_id(2)
    q = q_ref[0]
    row = lax.broadcasted_iota(jnp.int32, (tq, tk), 0)
    col = lax.broadcasted_iota(jnp.int32, (tq, tk), 1)
    tri = (row >= col).astype(BF16)
    past = col < row

    def tile(hh, j, carry, acc, masked):
        start = pl.multiple_of(j * tk, tk)
        k = k_ref[0, hh, pl.ds(start, tk), :]
        v = v_ref[0, hh, pl.ds(start, tk), :]
        z = _dot_nt(q, k)
        lf = -(jnp.maximum(z, 0.0) + jnp.log(1.0 + jnp.exp(-jnp.abs(z))))
        if masked:
            lf = jnp.where(past, lf, 0.0)
        hi = lf.astype(BF16)
        lo = (lf - hi.astype(F32)).astype(BF16)
        r = _dot(jnp.concatenate([hi, lo], axis=0), tri)
        c = r[:tq] + r[tq:]
        w = jnp.exp(z + c + carry)
        if masked:
            w = jnp.where(past, w, 0.0)
        acc = acc + _dot(w.astype(BF16), v)
        return carry + c[:, 0:1], acc

    o_pair = jnp.zeros((tq, LANES), F32)
    for hh in range(2):
        carry, acc = tile(hh, i, jnp.zeros((tq, 1), F32), jnp.zeros((tq, LANES), F32), True)
        carry, acc = lax.fori_loop(0, i, lambda n, ca: tile(hh, i - 1 - n, ca[0], ca[1], False), (carry, acc))
        o_pair = o_pair + acc
    o_ref[0] = _head_norm_gate(o_pair, g_ref[0], gate_ref[0]).astype(o_ref.dtype)


def _mla_kernel(q_ref, k_ref, v_ref, gate_ref, g_ref, o_ref):
    tq = q_ref.shape[1]
    tk = tq
    i = pl.program_id(2)
    row = lax.broadcasted_iota(jnp.int32, (tq, tk), 0)
    col = lax.broadcasted_iota(jnp.int32, (tq, tk), 1)
    visible = (col // CHUNK) <= (row // CHUNK)

    def tile(hh, q, j, m, l, acc, masked):
        start = pl.multiple_of(j * tk, tk)
        k = k_ref[0, hh, pl.ds(start, tk), :]
        v = v_ref[0, hh, pl.ds(start, tk), :]
        s = _dot_nt(q, k)
        if masked:
            s = jnp.where(visible, s, -jnp.inf)
        m_new = jnp.maximum(m, jnp.max(s, axis=-1, keepdims=True))
        alpha = jnp.exp(m - m_new)
        p = jnp.exp(s - m_new)
        l = alpha * l + jnp.sum(p, axis=-1, keepdims=True)
        acc = alpha * acc + _dot(p.astype(BF16), v)
        return m_new, l, acc

    o_pair = jnp.zeros((tq, LANES), F32)
    for hh in range(2):
        q = q_ref[0, :, hh * LANES:(hh + 1) * LANES]
        init = (jnp.full((tq, 1), -jnp.inf, F32), jnp.zeros((tq, 1), F32), jnp.zeros((tq, LANES), F32))
        state = tile(hh, q, i, *init, True)
        m, l, acc = lax.fori_loop(0, i, lambda n, st: tile(hh, q, n, *st, False), state)
        o_pair = o_pair + acc / l
    o_ref[0] = _head_norm_gate(o_pair, g_ref[0], gate_ref[0]).astype(o_ref.dtype)


def _attention(body, name, q, k, v, gate, g, q_width):
    B, S, _ = gate.shape
    tq = min(ATT_TILE, S)
    assert S % tq == 0
    return pl.pallas_call(
        body,
        grid=(B, N_PAIRS, S // tq),
        in_specs=[pl.BlockSpec((1, tq, q_width), lambda b, p, i: (b, i, p)),
                  pl.BlockSpec((1, 2, S, LANES), lambda b, p, i: (b, p, 0, 0)),
                  pl.BlockSpec((1, 2, S, LANES), lambda b, p, i: (b, p, 0, 0)),
                  pl.BlockSpec((1, tq, LANES), lambda b, p, i: (b, i, p)),
                  pl.BlockSpec((1, 1, LANES), lambda b, p, i: (p, 0, 0))],
        out_specs=pl.BlockSpec((1, tq, LANES), lambda b, p, i: (b, i, p)),
        out_shape=jax.ShapeDtypeStruct((B, S, D_GROUP), BF16),
        compiler_params=pltpu.CompilerParams(dimension_semantics=("arbitrary", "arbitrary", "arbitrary"),
                                             vmem_limit_bytes=VMEM_LIMIT),
        name=name,
    )(q, k, v, gate, g)


def _out_kernel(sb_ref, mla_ref, x_ref, p_ref, woa_ref, wob_ref, gpost_ref, wple_ref, gple_ref, wpg_ref, bpg_ref,
                o_ref):
    y = _dot(sb_ref[...], woa_ref[...]) + _dot(mla_ref[...], wob_ref[...])
    x1 = x_ref[...] + _rms(y, gpost_ref[...])
    ple = _rms(_dot(p_ref[...].astype(BF16), wple_ref[...]), gple_ref[...])
    gate = _sigmoid(_dot(x1.astype(BF16), wpg_ref[...]) + bpg_ref[...])
    o_ref[...] = x1 + ple * gate


def _output(sb_y, mla_y, x, p, woa, wob, gpost, wple, gple, wpg, bpg):
    B, S, D = x.shape
    tm = min(PROJ_ROWS, S)
    full = lambda a: pl.BlockSpec(a.shape, lambda b, t: (0,) * a.ndim)
    row = lambda w: pl.BlockSpec((None, tm, w), lambda b, t: (b, t, 0))
    weights = (woa, wob, gpost, wple, gple, wpg, bpg)
    return pl.pallas_call(
        _out_kernel,
        grid=(B, S // tm),
        in_specs=[row(D_GROUP), row(D_GROUP), row(D), row(p.shape[-1])] + [full(w) for w in weights],
        out_specs=row(D),
        out_shape=jax.ShapeDtypeStruct((B, S, D), F32),
        compiler_params=pltpu.CompilerParams(dimension_semantics=("arbitrary", "arbitrary"),
                                             vmem_limit_bytes=VMEM_LIMIT),
        name="proj_out",
    )(sb_y, mla_y, x, p, *weights)


def _rotate_half_cols(w):
    half = w.shape[-1] // 2
    return jnp.concatenate([-w[..., half:], w[..., :half]], axis=-1)


def _head_lanes(nope, rope):
    K = nope.shape[0]
    pad = jnp.zeros((K, N_HEADS, LANES - QK_NOPE_DIM - QK_ROPE_DIM), nope.dtype)
    return jnp.concatenate([nope, rope, pad], axis=-1).reshape(K, N_HEADS * LANES)


def _layer(x, p, pos, norm_pre_g, w_in, q_norm_g, w_uq, kv_norm_g, w_ukv, sb_out_norm_g, mla_out_norm_g, w_out,
           norm_post_g, w_ple, ple_norm_g, w_ple_gate, b_ple_gate):
    D = x.shape[-1]
    row = lambda a: a.reshape(1, -1).astype(F32)
    c = np.cumsum([0, D_GROUP, D_GROUP, D_GROUP, D_GROUP, Q_LORA_RANK, KV_LORA_RANK, QK_ROPE_DIM, D_GROUP])
    wq, wk, wv, wg, wcq, wckv, wkr, wmg = (w_in[:, c[n]:c[n + 1]] for n in range(8))

    def rope_lanes(w):
        return jnp.pad(w, ((0, 0), (QK_NOPE_DIM, LANES - QK_NOPE_DIM - QK_ROPE_DIM)))

    wkr2 = jnp.concatenate([rope_lanes(wkr), rope_lanes(_rotate_half_cols(wkr))], axis=-1)
    uq = w_uq.reshape(Q_LORA_RANK, N_HEADS, QK_NOPE_DIM + QK_ROPE_DIM)
    uq_nope, uq_rope = uq[..., :QK_NOPE_DIM], uq[..., QK_NOPE_DIM:]
    wuq2 = jnp.concatenate([_head_lanes(uq_nope, uq_rope),
                            _head_lanes(jnp.zeros_like(uq_nope), _rotate_half_cols(uq_rope))], axis=-1)
    ukv = w_ukv.reshape(KV_LORA_RANK, N_HEADS, QK_NOPE_DIM + HEAD_DIM)
    uk, uv = ukv[..., :QK_NOPE_DIM], ukv[..., QK_NOPE_DIM:]
    wukv2 = jnp.concatenate([_head_lanes(uk, jnp.zeros((KV_LORA_RANK, N_HEADS, QK_ROPE_DIM), uk.dtype)),
                             uv.reshape(KV_LORA_RANK, D_GROUP)], axis=-1)
    half = QK_ROPE_DIM // 2
    freq = ROPE_THETA ** (-jnp.arange(half, dtype=F32) / half)
    freq = jnp.pad(jnp.concatenate([freq, freq]), (QK_NOPE_DIM, LANES - QK_NOPE_DIM - QK_ROPE_DIM)).reshape(1, LANES)
    bf = lambda a: a.astype(BF16)

    sbq, sbk, sbv, sbg, mq, mk, mv, mg = _projections(
        x, pos.astype(F32)[..., None], freq, row(norm_pre_g), bf(wq), bf(wk), bf(wv), bf(wg), bf(wcq), bf(wckv),
        bf(wkr2), bf(wmg), row(q_norm_g), bf(wuq2), row(kv_norm_g), bf(wukv2))
    pair = lambda g: g.reshape(N_PAIRS, 1, LANES).astype(F32)
    sb_y = _attention(_sb_kernel, "sb_attn", sbq, sbk, sbv, sbg, pair(sb_out_norm_g), LANES)
    mla_y = _attention(_mla_kernel, "mla_attn", mq, mk, mv, mg, pair(mla_out_norm_g), 2 * LANES)
    return _output(sb_y, mla_y, x, p, bf(w_out[:D_GROUP]), bf(w_out[D_GROUP:]), row(norm_post_g), bf(w_ple),
                   row(ple_norm_g), bf(w_ple_gate), row(b_ple_gate))


def kernel(x, p, positions, norm_pre_g, w_in, q_norm_g, w_uq, kv_norm_g, w_ukv, sb_out_norm_g, mla_out_norm_g, w_out,
           norm_post_g, w_ple, ple_norm_g, w_ple_gate, b_ple_gate):
    for i in range(p.shape[0]):
        x = _layer(x, p[i], positions, norm_pre_g[i], w_in[i], q_norm_g[i], w_uq[i], kv_norm_g[i], w_ukv[i],
                   sb_out_norm_g[i], mla_out_norm_g[i], w_out[i], norm_post_g[i], w_ple[i], ple_norm_g[i],
                   w_ple_gate[i], b_ple_gate[i])
    return x
```

```python
import jax
import jax.numpy as jnp
import numpy as np
from jax import lax
from jax.experimental import pallas as pl
from jax.experimental.pallas import tpu as pltpu

HEAD_DIM = 64
N_HEADS = 8
N_PAIRS = N_HEADS // 2
LANES = 128
QK_NOPE_DIM = 64
QK_ROPE_DIM = 32
Q_LORA_RANK = 256
KV_LORA_RANK = 128
CHUNK = 64
ROPE_THETA = 10000.0
EPS = 1e-6
D_GROUP = N_HEADS * HEAD_DIM
LOG2E = 1.4426950408889634

PROJ_ROWS = 512
ATT_TQ = 512
ATT_TK = 256
VMEM_LIMIT = 56 * 1024 * 1024

F32 = jnp.float32
BF16 = jnp.bfloat16


def _rms(x, g):
    return x * lax.rsqrt(jnp.mean(x * x, axis=-1, keepdims=True) + EPS) * g


def _dot(a, b):
    return jnp.dot(a, b, preferred_element_type=F32)


def _dot_nt(a, b):
    return lax.dot_general(a, b, (((1,), (1,)), ((), ())), preferred_element_type=F32)


def _sigmoid(x):
    return 1.0 / (1.0 + jnp.exp(-x))


def _proj_kernel(x_ref, pos_ref, freq_ref, gpre_ref, wq_ref, wk_ref, wv_ref, wg_ref, wcq_ref, wckv_ref,
                 wkr_ref, wmg_ref, gq_ref, wuq_ref, gkv_ref, wukv_ref,
                 sbq_ref, sbk_ref, sbv_ref, sbg_ref, mq_ref, mk_ref, mv_ref, mg_ref):
    h = _rms(x_ref[...], gpre_ref[...]).astype(BF16)
    tk = sbv_ref.shape[-1]

    def store_transposed(a, out_ref):
        for p in range(N_PAIRS):
            at = a[:, p * LANES:(p + 1) * LANES].T.astype(BF16)
            for c in range(out_ref.shape[2]):
                out_ref[0, 2 * p, c] = at[:HEAD_DIM, c * tk:(c + 1) * tk]
                out_ref[0, 2 * p + 1, c] = at[HEAD_DIM:, c * tk:(c + 1) * tk]

    sbq_ref[0] = (_dot(h, wq_ref[...]) * (HEAD_DIM ** -0.5)).astype(BF16)
    sbk_ref[0] = _dot(h, wk_ref[...]).astype(BF16)
    store_transposed(_dot(h, wv_ref[...]), sbv_ref)
    sbg_ref[0] = _dot(h, wg_ref[...])
    mg_ref[0] = _dot(h, wmg_ref[...])

    ang = pos_ref[...] * freq_ref[...]
    cos, sin = jnp.cos(ang), jnp.sin(ang)

    cq = _rms(_dot(h, wcq_ref[...]), gq_ref[...]).astype(BF16)
    q2 = _dot(cq, wuq_ref[...])
    scale = (QK_NOPE_DIM + QK_ROPE_DIM) ** -0.5 * LOG2E
    half = N_HEADS * LANES
    for hd in range(N_HEADS):
        qa = q2[:, hd * LANES:(hd + 1) * LANES]
        qb = q2[:, half + hd * LANES:half + (hd + 1) * LANES]
        mq_ref[0, :, hd * LANES:(hd + 1) * LANES] = ((qa * cos + qb * sin) * scale).astype(BF16)

    ckv = _rms(_dot(h, wckv_ref[...]), gkv_ref[...]).astype(BF16)
    kv = _dot(ckv, wukv_ref[...])
    kr2 = _dot(h, wkr_ref[...])
    kr = kr2[:, :LANES] * cos + kr2[:, LANES:] * sin
    for hd in range(N_HEADS):
        mk_ref[0, hd] = (kv[:, hd * LANES:(hd + 1) * LANES] + kr).astype(BF16)
    store_transposed(kv[:, half:], mv_ref)


def _projections(x, pos, freq, gpre, wq, wk, wv, wg, wcq, wckv, wkr, wmg, gq, wuq, gkv, wukv):
    B, S, D = x.shape
    tm = min(PROJ_ROWS, S)
    tk = min(ATT_TK, S)
    assert S % tm == 0 and tm % tk == 0
    nt = S // tm
    full = lambda a: pl.BlockSpec(a.shape, lambda b, t: (0,) * a.ndim)
    row = lambda w: pl.BlockSpec((1, tm, w), lambda b, t: (b, t, 0))
    head = pl.BlockSpec((1, N_HEADS, tm, LANES), lambda b, t: (b, 0, t, 0))
    vt = pl.BlockSpec((1, N_HEADS, tm // tk, HEAD_DIM, tk), lambda b, t: (b, 0, t, 0, 0))
    vt_shape = jax.ShapeDtypeStruct((B, N_HEADS, S // tk, HEAD_DIM, tk), BF16)
    weights = (freq, gpre, wq, wk, wv, wg, wcq, wckv, wkr, wmg, gq, wuq, gkv, wukv)
    return pl.pallas_call(
        _proj_kernel,
        grid=(B, nt),
        in_specs=[pl.BlockSpec((None, tm, D), lambda b, t: (b, t, 0)),
                  pl.BlockSpec((None, tm, 1), lambda b, t: (b, t, 0))] + [full(w) for w in weights],
        out_specs=[row(D_GROUP), row(D_GROUP), vt, row(D_GROUP), row(N_HEADS * LANES), head, vt, row(D_GROUP)],
        out_shape=[jax.ShapeDtypeStruct((B, S, D_GROUP), BF16), jax.ShapeDtypeStruct((B, S, D_GROUP), BF16), vt_shape,
                   jax.ShapeDtypeStruct((B, S, D_GROUP), F32),
                   jax.ShapeDtypeStruct((B, S, N_HEADS * LANES), BF16),
                   jax.ShapeDtypeStruct((B, N_HEADS, S, LANES), BF16), vt_shape,
                   jax.ShapeDtypeStruct((B, S, D_GROUP), F32)],
        compiler_params=pltpu.CompilerParams(dimension_semantics=("arbitrary", "arbitrary"),
                                             vmem_limit_bytes=VMEM_LIMIT),
        name="proj_in",
    )(x, pos, *weights)


def _head_norm_gate(o_t, g, gate):
    normed = [o * lax.rsqrt(jnp.mean(o * o, axis=0, keepdims=True) + EPS) for o in o_t]
    o_pair = jnp.concatenate(normed, axis=0).T
    return o_pair * g * (gate * _sigmoid(gate))


def _sb_kernel(q_ref, k_ref, v_ref, gate_ref, g_ref, o_ref):
    tq = q_ref.shape[1]
    tk = v_ref.shape[-1]
    ratio = tq // tk
    i = pl.program_id(2)
    lane = lax.broadcasted_iota(jnp.int32, (tq, LANES), 1)
    q = q_ref[0]
    qs = [jnp.where(lane < HEAD_DIM, q, 0), jnp.where(lane < HEAD_DIM, 0, q)]
    key = lax.broadcasted_iota(jnp.int32, (tk, tq), 0)
    qry = lax.broadcasted_iota(jnp.int32, (tk, tq), 1)
    ksum = lax.broadcasted_iota(jnp.int32, (tk, tk), 1)
    upper = (ksum >= lax.broadcasted_iota(jnp.int32, (tk, tk), 0)).astype(BF16)

    def scores(j):
        start = pl.multiple_of(j * tk, tk)
        k = k_ref[0, pl.ds(start, tk), :]
        return [_dot_nt(k, qs[hh]) for hh in range(2)]

    def tile(hh, j, z, carry, acc, past):
        lf = -(jnp.maximum(z, 0.0) + jnp.log(1.0 + jnp.exp(-jnp.abs(z))))
        if past is not None:
            lf = jnp.where(past, lf, 0.0)
        hi = lf.astype(BF16)
        lo = (lf - hi.astype(F32)).astype(BF16)
        c = _dot(upper, hi) + _dot(upper, lo)
        w = jnp.exp(z + c + carry)
        if past is not None:
            w = jnp.where(past, w, 0.0)
        acc = acc + _dot(v_ref[0, hh, j], w.astype(BF16))
        return carry + c[0:1, :], acc

    def step(j, st, past):
        zs = scores(j)
        return tuple(tile(hh, j, zs[hh], *st[hh], past) for hh in range(2))

    zero = (jnp.zeros((1, tq), F32), jnp.zeros((HEAD_DIM, tq), F32))
    state = (zero, zero)
    for d in reversed(range(ratio)):
        state = step(i * ratio + d, state, (key + d * tk) < qry)
    state = lax.fori_loop(0, i * ratio, lambda n, st: step(i * ratio - 1 - n, st, None), state)
    o_ref[0] = _head_norm_gate([state[0][1], state[1][1]], g_ref[0], gate_ref[0]).astype(o_ref.dtype)


def _mla_kernel(q_ref, k_ref, v_ref, gate_ref, g_ref, o_ref):
    tq = q_ref.shape[1]
    tk = v_ref.shape[-1]
    ratio = tq // tk
    i = pl.program_id(2)
    key = lax.broadcasted_iota(jnp.int32, (tk, tq), 0)
    qry = lax.broadcasted_iota(jnp.int32, (tk, tq), 1)
    qs = [q_ref[0, :, hh * LANES:(hh + 1) * LANES] for hh in range(2)]

    def scores(j):
        start = pl.multiple_of(j * tk, tk)
        return [_dot_nt(k_ref[0, hh, pl.ds(start, tk), :], qs[hh]) for hh in range(2)]

    def tile(hh, j, s, m, l, acc, visible):
        if visible is not None:
            s = jnp.where(visible, s, -jnp.inf)
        m_new = jnp.maximum(m, jnp.max(s, axis=0, keepdims=True))
        alpha = jnp.exp2(m - m_new)
        p = jnp.exp2(s - m_new)
        l = alpha * l + jnp.sum(p, axis=0, keepdims=True)
        acc = alpha * acc + _dot(v_ref[0, hh, j], p.astype(BF16))
        return m_new, l, acc

    def step(j, st, visible):
        ss = scores(j)
        return tuple(tile(hh, j, ss[hh], *st[hh], visible) for hh in range(2))

    init = (jnp.full((1, tq), -jnp.inf, F32), jnp.zeros((1, tq), F32), jnp.zeros((HEAD_DIM, tq), F32))
    state = (init, init)
    for d in range(ratio):
        state = step(i * ratio + d, state, ((key + d * tk) // CHUNK) <= (qry // CHUNK))
    state = lax.fori_loop(0, i * ratio, lambda n, st: step(n, st, None), state)
    o_t = [state[hh][2] / state[hh][1] for hh in range(2)]
    o_ref[0] = _head_norm_gate(o_t, g_ref[0], gate_ref[0]).astype(o_ref.dtype)


def _attention(body, name, q, k, v, gate, g, q_width, k_spec):
    B, S, _ = gate.shape
    tq = min(ATT_TQ, S)
    assert S % tq == 0 and tq % v.shape[-1] == 0
    return pl.pallas_call(
        body,
        grid=(B, N_PAIRS, S // tq),
        in_specs=[pl.BlockSpec((1, tq, q_width), lambda b, p, i: (b, i, p)),
                  k_spec,
                  pl.BlockSpec((1, 2) + v.shape[2:], lambda b, p, i: (b, p, 0, 0, 0)),
                  pl.BlockSpec((1, tq, LANES), lambda b, p, i: (b, i, p)),
                  pl.BlockSpec((1, 1, LANES), lambda b, p, i: (p, 0, 0))],
        out_specs=pl.BlockSpec((1, tq, LANES), lambda b, p, i: (b, i, p)),
        out_shape=jax.ShapeDtypeStruct((B, S, D_GROUP), BF16),
        compiler_params=pltpu.CompilerParams(dimension_semantics=("arbitrary", "arbitrary", "arbitrary"),
                                             vmem_limit_bytes=VMEM_LIMIT),
        name=name,
    )(q, k, v, gate, g)


def _out_kernel(sb_ref, mla_ref, x_ref, p_ref, woa_ref, wob_ref, gpost_ref, wple_ref, gple_ref, wpg_ref, bpg_ref,
                o_ref):
    y = _dot(sb_ref[...], woa_ref[...]) + _dot(mla_ref[...], wob_ref[...])
    x1 = x_ref[...] + _rms(y, gpost_ref[...])
    ple = _rms(_dot(p_ref[...].astype(BF16), wple_ref[...]), gple_ref[...])
    gate = _sigmoid(_dot(x1.astype(BF16), wpg_ref[...]) + bpg_ref[...])
    o_ref[...] = x1 + ple * gate


def _output(sb_y, mla_y, x, p, woa, wob, gpost, wple, gple, wpg, bpg):
    B, S, D = x.shape
    tm = min(PROJ_ROWS, S)
    full = lambda a: pl.BlockSpec(a.shape, lambda b, t: (0,) * a.ndim)
    row = lambda w: pl.BlockSpec((None, tm, w), lambda b, t: (b, t, 0))
    weights = (woa, wob, gpost, wple, gple, wpg, bpg)
    return pl.pallas_call(
        _out_kernel,
        grid=(B, S // tm),
        in_specs=[row(D_GROUP), row(D_GROUP), row(D), row(p.shape[-1])] + [full(w) for w in weights],
        out_specs=row(D),
        out_shape=jax.ShapeDtypeStruct((B, S, D), F32),
        compiler_params=pltpu.CompilerParams(dimension_semantics=("arbitrary", "arbitrary"),
                                             vmem_limit_bytes=VMEM_LIMIT),
        name="proj_out",
    )(sb_y, mla_y, x, p, *weights)


def _rotate_half_cols(w):
    half = w.shape[-1] // 2
    return jnp.concatenate([-w[..., half:], w[..., :half]], axis=-1)


def _head_lanes(nope, rope):
    K = nope.shape[0]
    pad = jnp.zeros((K, N_HEADS, LANES - QK_NOPE_DIM - QK_ROPE_DIM), nope.dtype)
    return jnp.concatenate([nope, rope, pad], axis=-1).reshape(K, N_HEADS * LANES)


def _layer(x, p, pos, norm_pre_g, w_in, q_norm_g, w_uq, kv_norm_g, w_ukv, sb_out_norm_g, mla_out_norm_g, w_out,
           norm_post_g, w_ple, ple_norm_g, w_ple_gate, b_ple_gate):
    S = x.shape[1]
    row = lambda a: a.reshape(1, -1).astype(F32)
    c = np.cumsum([0, D_GROUP, D_GROUP, D_GROUP, D_GROUP, Q_LORA_RANK, KV_LORA_RANK, QK_ROPE_DIM, D_GROUP])
    wq, wk, wv, wg, wcq, wckv, wkr, wmg = (w_in[:, c[n]:c[n + 1]] for n in range(8))

    def rope_lanes(w):
        return jnp.pad(w, ((0, 0), (QK_NOPE_DIM, LANES - QK_NOPE_DIM - QK_ROPE_DIM)))

    wkr2 = jnp.concatenate([rope_lanes(wkr), rope_lanes(_rotate_half_cols(wkr))], axis=-1)
    uq = w_uq.reshape(Q_LORA_RANK, N_HEADS, QK_NOPE_DIM + QK_ROPE_DIM)
    uq_nope, uq_rope = uq[..., :QK_NOPE_DIM], uq[..., QK_NOPE_DIM:]
    wuq2 = jnp.concatenate([_head_lanes(uq_nope, uq_rope),
                            _head_lanes(jnp.zeros_like(uq_nope), _rotate_half_cols(uq_rope))], axis=-1)
    ukv = w_ukv.reshape(KV_LORA_RANK, N_HEADS, QK_NOPE_DIM + HEAD_DIM)
    uk, uv = ukv[..., :QK_NOPE_DIM], ukv[..., QK_NOPE_DIM:]
    wukv2 = jnp.concatenate([_head_lanes(uk, jnp.zeros((KV_LORA_RANK, N_HEADS, QK_ROPE_DIM), uk.dtype)),
                             uv.reshape(KV_LORA_RANK, D_GROUP)], axis=-1)
    half = QK_ROPE_DIM // 2
    freq = ROPE_THETA ** (-jnp.arange(half, dtype=F32) / half)
    freq = jnp.pad(jnp.concatenate([freq, freq]), (QK_NOPE_DIM, LANES - QK_NOPE_DIM - QK_ROPE_DIM)).reshape(1, LANES)
    bf = lambda a: a.astype(BF16)

    sbq, sbk, sbv, sbg, mq, mk, mv, mg = _projections(
        x, pos.astype(F32)[..., None], freq, row(norm_pre_g), bf(wq), bf(wk), bf(wv), bf(wg), bf(wcq), bf(wckv),
        bf(wkr2), bf(wmg), row(q_norm_g), bf(wuq2), row(kv_norm_g), bf(wukv2))
    pair = lambda g: g.reshape(N_PAIRS, 1, LANES).astype(F32)
    sb_k_spec = pl.BlockSpec((1, S, LANES), lambda b, p, i: (b, 0, p))
    mla_k_spec = pl.BlockSpec((1, 2, S, LANES), lambda b, p, i: (b, p, 0, 0))
    sb_y = _attention(_sb_kernel, "sb_attn", sbq, sbk, sbv, sbg, pair(sb_out_norm_g), LANES, sb_k_spec)
    mla_y = _attention(_mla_kernel, "mla_attn", mq, mk, mv, mg, pair(mla_out_norm_g), 2 * LANES, mla_k_spec)
    return _output(sb_y, mla_y, x, p, bf(w_out[:D_GROUP]), bf(w_out[D_GROUP:]), row(norm_post_g), bf(w_ple),
                   row(ple_norm_g), bf(w_ple_gate), row(b_ple_gate))


def kernel(x, p, positions, norm_pre_g, w_in, q_norm_g, w_uq, kv_norm_g, w_ukv, sb_out_norm_g, mla_out_norm_g, w_out,
           norm_post_g, w_ple, ple_norm_g, w_ple_gate, b_ple_gate):
    for i in range(p.shape[0]):
        x = _layer(x, p[i], positions, norm_pre_g[i], w_in[i], q_norm_g[i], w_uq[i], kv_norm_g[i], w_ukv[i],
                   sb_out_norm_g[i], mla_out_norm_g[i], w_out[i], norm_post_g[i], w_ple[i], ple_norm_g[i],
                   w_ple_gate[i], b_ple_gate[i])
    return x
```

```python
import jax
import jax.numpy as jnp
import numpy as np
from jax import lax
from jax.experimental import pallas as pl
from jax.experimental.pallas import tpu as pltpu

HEAD_DIM = 64
N_HEADS = 8
N_PAIRS = N_HEADS // 2
LANES = 128
QK_NOPE_DIM = 64
QK_ROPE_DIM = 32
Q_LORA_RANK = 256
KV_LORA_RANK = 128
CHUNK = 64
ROPE_THETA = 10000.0
EPS = 1e-6
D_GROUP = N_HEADS * HEAD_DIM
LOG2E = 1.4426950408889634

PROJ_ROWS = 512
ATT_TK = 256
SB_TQ = 256
MLA_TQ = 512
LOOKAHEAD = 3
SKIP_LOG2 = 150.0
EXP2_CLAMP = 126.0
VMEM_LIMIT = 56 * 1024 * 1024

F32 = jnp.float32
BF16 = jnp.bfloat16


def _rms(x, g):
    return x * lax.rsqrt(jnp.mean(x * x, axis=-1, keepdims=True) + EPS) * g


def _dot(a, b):
    return jnp.dot(a, b, preferred_element_type=F32)


def _dot_nt(a, b):
    return lax.dot_general(a, b, (((1,), (1,)), ((), ())), preferred_element_type=F32)


def _sigmoid(x):
    return 1.0 / (1.0 + jnp.exp(-x))


def _proj_kernel(x_ref, pos_ref, freq_ref, gpre_ref, wq_ref, wk_ref, wv_ref, wg_ref, wcq_ref, wckv_ref,
                 wkr_ref, wmg_ref, gq_ref, wuq_ref, gkv_ref, wukv_ref,
                 sbq_ref, sbk_ref, sbv_ref, sbg_ref, mq_ref, mk_ref, mv_ref, mg_ref):
    h = _rms(x_ref[...], gpre_ref[...]).astype(BF16)
    tk = sbv_ref.shape[-1]

    def store_transposed(a, out_ref):
        for p in range(N_PAIRS):
            at = a[:, p * LANES:(p + 1) * LANES].T.astype(BF16)
            for c in range(out_ref.shape[2]):
                out_ref[0, 2 * p, c] = at[:HEAD_DIM, c * tk:(c + 1) * tk]
                out_ref[0, 2 * p + 1, c] = at[HEAD_DIM:, c * tk:(c + 1) * tk]

    sbq_ref[0] = (_dot(h, wq_ref[...]) * (HEAD_DIM ** -0.5 * LOG2E)).astype(BF16)
    sbk_ref[0] = _dot(h, wk_ref[...]).astype(BF16)
    store_transposed(_dot(h, wv_ref[...]), sbv_ref)
    sbg_ref[0] = _dot(h, wg_ref[...])
    mg_ref[0] = _dot(h, wmg_ref[...])

    ang = pos_ref[...] * freq_ref[...]
    cos, sin = jnp.cos(ang), jnp.sin(ang)

    cq = _rms(_dot(h, wcq_ref[...]), gq_ref[...]).astype(BF16)
    q2 = _dot(cq, wuq_ref[...])
    scale = (QK_NOPE_DIM + QK_ROPE_DIM) ** -0.5 * LOG2E
    half = N_HEADS * LANES
    for hd in range(N_HEADS):
        qa = q2[:, hd * LANES:(hd + 1) * LANES]
        qb = q2[:, half + hd * LANES:half + (hd + 1) * LANES]
        mq_ref[0, :, hd * LANES:(hd + 1) * LANES] = ((qa * cos + qb * sin) * scale).astype(BF16)

    ckv = _rms(_dot(h, wckv_ref[...]), gkv_ref[...]).astype(BF16)
    kv = _dot(ckv, wukv_ref[...])
    kr2 = _dot(h, wkr_ref[...])
    kr = kr2[:, :LANES] * cos + kr2[:, LANES:] * sin
    for hd in range(N_HEADS):
        mk_ref[0, hd] = (kv[:, hd * LANES:(hd + 1) * LANES] + kr).astype(BF16)
    store_transposed(kv[:, half:], mv_ref)


def _projections(x, pos, freq, gpre, wq, wk, wv, wg, wcq, wckv, wkr, wmg, gq, wuq, gkv, wukv):
    B, S, D = x.shape
    tm = min(PROJ_ROWS, S)
    tk = min(ATT_TK, S)
    assert S % tm == 0 and tm % tk == 0
    nt = S // tm
    full = lambda a: pl.BlockSpec(a.shape, lambda b, t: (0,) * a.ndim)
    row = lambda w: pl.BlockSpec((1, tm, w), lambda b, t: (b, t, 0))
    head = pl.BlockSpec((1, N_HEADS, tm, LANES), lambda b, t: (b, 0, t, 0))
    vt = pl.BlockSpec((1, N_HEADS, tm // tk, HEAD_DIM, tk), lambda b, t: (b, 0, t, 0, 0))
    vt_shape = jax.ShapeDtypeStruct((B, N_HEADS, S // tk, HEAD_DIM, tk), BF16)
    weights = (freq, gpre, wq, wk, wv, wg, wcq, wckv, wkr, wmg, gq, wuq, gkv, wukv)
    return pl.pallas_call(
        _proj_kernel,
        grid=(B, nt),
        in_specs=[pl.BlockSpec((None, tm, D), lambda b, t: (b, t, 0)),
                  pl.BlockSpec((None, tm, 1), lambda b, t: (b, t, 0))] + [full(w) for w in weights],
        out_specs=[row(D_GROUP), row(D_GROUP), vt, row(D_GROUP), row(N_HEADS * LANES), head, vt, row(D_GROUP)],
        out_shape=[jax.ShapeDtypeStruct((B, S, D_GROUP), BF16), jax.ShapeDtypeStruct((B, S, D_GROUP), BF16), vt_shape,
                   jax.ShapeDtypeStruct((B, S, D_GROUP), F32),
                   jax.ShapeDtypeStruct((B, S, N_HEADS * LANES), BF16),
                   jax.ShapeDtypeStruct((B, N_HEADS, S, LANES), BF16), vt_shape,
                   jax.ShapeDtypeStruct((B, S, D_GROUP), F32)],
        compiler_params=pltpu.CompilerParams(dimension_semantics=("arbitrary", "arbitrary"),
                                             vmem_limit_bytes=VMEM_LIMIT),
        name="proj_in",
    )(x, pos, *weights)


def _store_heads(o_t, g_ref, gate_ref, o_ref):
    for p in range(N_PAIRS):
        normed = [o * lax.rsqrt(jnp.mean(o * o, axis=0, keepdims=True) + EPS) for o in o_t[2 * p:2 * p + 2]]
        o_pair = jnp.concatenate(normed, axis=0).T
        cols = slice(p * LANES, (p + 1) * LANES)
        gate = gate_ref[0, :, cols]
        o_ref[0, :, cols] = (o_pair * g_ref[:, cols] * (gate * _sigmoid(gate))).astype(o_ref.dtype)


def _pipelined_heads(score, first, second):
    s = [score(h) for h in range(LOOKAHEAD)]
    mid = []
    for h in range(N_HEADS):
        mid.append(first(h, s[h]))
        if h + LOOKAHEAD < N_HEADS:
            s.append(score(h + LOOKAHEAD))
        if second is not None and h > 0:
            second(h - 1, s[h - 1], mid[h - 1])
    if second is not None:
        second(N_HEADS - 1, s[-1], mid[-1])


def _sb_kernel(q_ref, k_ref, v_ref, gate_ref, g_ref, o_ref, acc_ref, carry_ref):
    tq = q_ref.shape[1]
    tk = v_ref.shape[-1]
    ratio = tq // tk
    i = pl.program_id(1)
    low_half = lax.broadcasted_iota(jnp.int32, (tq, LANES), 1) < HEAD_DIM
    qs = []
    for p in range(N_PAIRS):
        qp = q_ref[0, :, p * LANES:(p + 1) * LANES]
        qs += [jnp.where(low_half, qp, 0), jnp.where(low_half, 0, qp)]
    key = lax.broadcasted_iota(jnp.int32, (tk, tq), 0)
    qry = lax.broadcasted_iota(jnp.int32, (tk, tq), 1)
    ksum = lax.broadcasted_iota(jnp.int32, (tk, 2 * tk), 1) % tk
    upper2 = (ksum >= lax.broadcasted_iota(jnp.int32, (tk, 2 * tk), 0)).astype(BF16)

    acc_ref[...] = jnp.zeros_like(acc_ref)
    carry_ref[...] = jnp.zeros_like(carry_ref)

    def step(j, past):
        start = pl.multiple_of(j * tk, tk)

        def score(h):
            pair = slice((h // 2) * LANES, (h // 2 + 1) * LANES)
            return _dot_nt(k_ref[0, pl.ds(start, tk), pair], qs[h])

        def suffix(h, z):
            sp = jnp.maximum(jnp.log2(1.0 + jnp.exp2(jnp.minimum(z, EXP2_CLAMP))), z)
            if past is not None:
                sp = jnp.where(past, sp, 0.0)
            hi = sp.astype(BF16)
            lo = (sp - hi.astype(F32)).astype(BF16)
            return _dot(upper2, jnp.concatenate([hi, lo], axis=0))

        def weigh(h, z, c):
            carry = carry_ref[h:h + 1, :]
            w = jnp.exp2(z - c - carry)
            if past is not None:
                w = jnp.where(past, w, 0.0)
            acc_ref[h] += _dot(v_ref[0, h, j], w.astype(BF16))
            carry_ref[h:h + 1, :] = carry + c[0:1, :]

        _pipelined_heads(score, suffix, weigh)

    for d in reversed(range(ratio)):
        step(i * ratio + d, (key + d * tk) < qry)

    n_off = i * ratio

    def body(state):
        n, _ = state
        step(n_off - 1 - n, None)
        return n + 1, jnp.min(carry_ref[...])

    lax.while_loop(lambda st: (st[0] < n_off) & (st[1] < SKIP_LOG2), body, (jnp.int32(0), jnp.float32(0.0)))
    _store_heads([acc_ref[h] for h in range(N_HEADS)], g_ref, gate_ref, o_ref)


def _mla_kernel(q_ref, k_ref, v_ref, gate_ref, g_ref, o_ref, acc_ref, m_ref, l_ref):
    tq = q_ref.shape[1]
    tk = v_ref.shape[-1]
    ratio = tq // tk
    i = pl.program_id(1)
    key = lax.broadcasted_iota(jnp.int32, (tk, tq), 0)
    qry = lax.broadcasted_iota(jnp.int32, (tk, tq), 1)

    acc_ref[...] = jnp.zeros_like(acc_ref)
    l_ref[...] = jnp.zeros_like(l_ref)
    m_ref[...] = jnp.full(m_ref.shape, -jnp.inf, F32)

    def step(j, visible):
        start = pl.multiple_of(j * tk, tk)

        def score(h):
            return _dot_nt(k_ref[0, h, pl.ds(start, tk), :], q_ref[0, :, h * LANES:(h + 1) * LANES])

        def update(h, s):
            if visible is not None:
                s = jnp.where(visible, s, -jnp.inf)
            m = m_ref[h:h + 1, :]
            m_new = jnp.maximum(m, jnp.max(s, axis=0, keepdims=True))
            alpha = jnp.exp2(m - m_new)
            p = jnp.exp2(s - m_new)
            l_ref[h:h + 1, :] = alpha * l_ref[h:h + 1, :] + jnp.sum(p, axis=0, keepdims=True)
            m_ref[h:h + 1, :] = m_new
            acc_ref[h] = alpha * acc_ref[h] + _dot(v_ref[0, h, j], p.astype(BF16))

        _pipelined_heads(score, update, None)

    for d in range(ratio):
        step(i * ratio + d, ((key + d * tk) // CHUNK) <= (qry // CHUNK))

    def body(n, _):
        step(n, None)
        return 0

    lax.fori_loop(0, i * ratio, body, 0)
    _store_heads([acc_ref[h] / l_ref[h:h + 1, :] for h in range(N_HEADS)], g_ref, gate_ref, o_ref)


def _attention(body, name, tq, q, k, v, gate, g, k_spec, n_stats):
    B, S, _ = gate.shape
    tq = min(tq, S)
    assert S % tq == 0 and tq % v.shape[-1] == 0
    return pl.pallas_call(
        body,
        grid=(B, S // tq),
        in_specs=[pl.BlockSpec((1, tq, q.shape[-1]), lambda b, i: (b, i, 0)),
                  k_spec,
                  pl.BlockSpec((1,) + v.shape[1:], lambda b, i: (b, 0, 0, 0, 0)),
                  pl.BlockSpec((1, tq, D_GROUP), lambda b, i: (b, i, 0)),
                  pl.BlockSpec((1, D_GROUP), lambda b, i: (0, 0))],
        out_specs=pl.BlockSpec((1, tq, D_GROUP), lambda b, i: (b, i, 0)),
        out_shape=jax.ShapeDtypeStruct((B, S, D_GROUP), BF16),
        scratch_shapes=[pltpu.VMEM((N_HEADS, HEAD_DIM, tq), F32)] + [pltpu.VMEM((N_HEADS, tq), F32)] * n_stats,
        compiler_params=pltpu.CompilerParams(dimension_semantics=("arbitrary", "arbitrary"),
                                             vmem_limit_bytes=VMEM_LIMIT),
        name=name,
    )(q, k, v, gate, g)


def _out_kernel(sb_ref, mla_ref, x_ref, p_ref, woa_ref, wob_ref, gpost_ref, wple_ref, gple_ref, wpg_ref, bpg_ref,
                o_ref):
    y = _dot(sb_ref[...], woa_ref[...]) + _dot(mla_ref[...], wob_ref[...])
    x1 = x_ref[...] + _rms(y, gpost_ref[...])
    ple = _rms(_dot(p_ref[...].astype(BF16), wple_ref[...]), gple_ref[...])
    gate = _sigmoid(_dot(x1.astype(BF16), wpg_ref[...]) + bpg_ref[...])
    o_ref[...] = x1 + ple * gate


def _output(sb_y, mla_y, x, p, woa, wob, gpost, wple, gple, wpg, bpg):
    B, S, D = x.shape
    tm = min(PROJ_ROWS, S)
    full = lambda a: pl.BlockSpec(a.shape, lambda b, t: (0,) * a.ndim)
    row = lambda w: pl.BlockSpec((None, tm, w), lambda b, t: (b, t, 0))
    weights = (woa, wob, gpost, wple, gple, wpg, bpg)
    return pl.pallas_call(
        _out_kernel,
        grid=(B, S // tm),
        in_specs=[row(D_GROUP), row(D_GROUP), row(D), row(p.shape[-1])] + [full(w) for w in weights],
        out_specs=row(D),
        out_shape=jax.ShapeDtypeStruct((B, S, D), F32),
        compiler_params=pltpu.CompilerParams(dimension_semantics=("arbitrary", "arbitrary"),
                                             vmem_limit_bytes=VMEM_LIMIT),
        name="proj_out",
    )(sb_y, mla_y, x, p, *weights)


def _rotate_half_cols(w):
    half = w.shape[-1] // 2
    return jnp.concatenate([-w[..., half:], w[..., :half]], axis=-1)


def _head_lanes(nope, rope):
    K = nope.shape[0]
    pad = jnp.zeros((K, N_HEADS, LANES - QK_NOPE_DIM - QK_ROPE_DIM), nope.dtype)
    return jnp.concatenate([nope, rope, pad], axis=-1).reshape(K, N_HEADS * LANES)


def _layer(x, p, pos, norm_pre_g, w_in, q_norm_g, w_uq, kv_norm_g, w_ukv, sb_out_norm_g, mla_out_norm_g, w_out,
           norm_post_g, w_ple, ple_norm_g, w_ple_gate, b_ple_gate):
    S = x.shape[1]
    row = lambda a: a.reshape(1, -1).astype(F32)
    c = np.cumsum([0, D_GROUP, D_GROUP, D_GROUP, D_GROUP, Q_LORA_RANK, KV_LORA_RANK, QK_ROPE_DIM, D_GROUP])
    wq, wk, wv, wg, wcq, wckv, wkr, wmg = (w_in[:, c[n]:c[n + 1]] for n in range(8))

    def rope_lanes(w):
        return jnp.pad(w, ((0, 0), (QK_NOPE_DIM, LANES - QK_NOPE_DIM - QK_ROPE_DIM)))

    wkr2 = jnp.concatenate([rope_lanes(wkr), rope_lanes(_rotate_half_cols(wkr))], axis=-1)
    uq = w_uq.reshape(Q_LORA_RANK, N_HEADS, QK_NOPE_DIM + QK_ROPE_DIM)
    uq_nope, uq_rope = uq[..., :QK_NOPE_DIM], uq[..., QK_NOPE_DIM:]
    wuq2 = jnp.concatenate([_head_lanes(uq_nope, uq_rope),
                            _head_lanes(jnp.zeros_like(uq_nope), _rotate_half_cols(uq_rope))], axis=-1)
    ukv = w_ukv.reshape(KV_LORA_RANK, N_HEADS, QK_NOPE_DIM + HEAD_DIM)
    uk, uv = ukv[..., :QK_NOPE_DIM], ukv[..., QK_NOPE_DIM:]
    wukv2 = jnp.concatenate([_head_lanes(uk, jnp.zeros((KV_LORA_RANK, N_HEADS, QK_ROPE_DIM), uk.dtype)),
                             uv.reshape(KV_LORA_RANK, D_GROUP)], axis=-1)
    half = QK_ROPE_DIM // 2
    freq = ROPE_THETA ** (-jnp.arange(half, dtype=F32) / half)
    freq = jnp.pad(jnp.concatenate([freq, freq]), (QK_NOPE_DIM, LANES - QK_NOPE_DIM - QK_ROPE_DIM)).reshape(1, LANES)
    bf = lambda a: a.astype(BF16)

    sbq, sbk, sbv, sbg, mq, mk, mv, mg = _projections(
        x, pos.astype(F32)[..., None], freq, row(norm_pre_g), bf(wq), bf(wk), bf(wv), bf(wg), bf(wcq), bf(wckv),
        bf(wkr2), bf(wmg), row(q_norm_g), bf(wuq2), row(kv_norm_g), bf(wukv2))
    sb_k_spec = pl.BlockSpec((1, S, D_GROUP), lambda b, i: (b, 0, 0))
    mla_k_spec = pl.BlockSpec((1, N_HEADS, S, LANES), lambda b, i: (b, 0, 0, 0))
    sb_y = _attention(_sb_kernel, "sb_attn", SB_TQ, sbq, sbk, sbv, sbg, row(sb_out_norm_g), sb_k_spec, 1)
    mla_y = _attention(_mla_kernel, "mla_attn", MLA_TQ, mq, mk, mv, mg, row(mla_out_norm_g), mla_k_spec, 2)
    return _output(sb_y, mla_y, x, p, bf(w_out[:D_GROUP]), bf(w_out[D_GROUP:]), row(norm_post_g), bf(w_ple),
                   row(ple_norm_g), bf(w_ple_gate), row(b_ple_gate))


def kernel(x, p, positions, norm_pre_g, w_in, q_norm_g, w_uq, kv_norm_g, w_ukv, sb_out_norm_g, mla_out_norm_g, w_out,
           norm_post_g, w_ple, ple_norm_g, w_ple_gate, b_ple_gate):
    for i in range(p.shape[0]):
        x = _layer(x, p[i], positions, norm_pre_g[i], w_in[i], q_norm_g[i], w_uq[i], kv_norm_g[i], w_ukv[i],
                   sb_out_norm_g[i], mla_out_norm_g[i], w_out[i], norm_post_g[i], w_ple[i], ple_norm_g[i],
                   w_ple_gate[i], b_ple_gate[i])
    return x
```

```python
import jax
import jax.numpy as jnp
import numpy as np
from jax import lax
from jax.experimental import pallas as pl
from jax.experimental.pallas import tpu as pltpu

HEAD_DIM = 64
N_HEADS = 8
N_PAIRS = N_HEADS // 2
LANES = 128
QK_NOPE_DIM = 64
QK_ROPE_DIM = 32
Q_LORA_RANK = 256
KV_LORA_RANK = 128
CHUNK = 64
ROPE_THETA = 10000.0
EPS = 1e-6
D_GROUP = N_HEADS * HEAD_DIM
LOG2E = 1.4426950408889634

PROJ_ROWS = 512
SB_TK = 256
SB_TQ = 256
MLA_TK = 512
MLA_TQ = 512
SUM_ROWS = 16
LOOKAHEAD = 3
SKIP_LOG2 = 150.0
EXP2_CLAMP = 126.0
VMEM_LIMIT = 56 * 1024 * 1024

F32 = jnp.float32
BF16 = jnp.bfloat16


def _rms(x, g):
    return x * lax.rsqrt(jnp.mean(x * x, axis=-1, keepdims=True) + EPS) * g


def _dot(a, b):
    return jnp.dot(a, b, preferred_element_type=F32)


def _dot_nt(a, b):
    return lax.dot_general(a, b, (((1,), (1,)), ((), ())), preferred_element_type=F32)


def _sigmoid(x):
    return 1.0 / (1.0 + jnp.exp(-x))


def _proj_kernel(x_ref, pos_ref, freq_ref, gpre_ref, wq_ref, wk_ref, wv_ref, wg_ref, wcq_ref, wckv_ref,
                 wkr_ref, wmg_ref, gq_ref, wuq_ref, gkv_ref, wukv_ref,
                 sbq_ref, sbk_ref, sbv_ref, sbg_ref, mq_ref, mk_ref, mv_ref, mg_ref):
    h = _rms(x_ref[...], gpre_ref[...]).astype(BF16)

    def store_transposed(a, out_ref):
        tk = out_ref.shape[-1]
        for p in range(N_PAIRS):
            at = a[:, p * LANES:(p + 1) * LANES].T.astype(BF16)
            for c in range(out_ref.shape[2]):
                out_ref[0, 2 * p, c] = at[:HEAD_DIM, c * tk:(c + 1) * tk]
                out_ref[0, 2 * p + 1, c] = at[HEAD_DIM:, c * tk:(c + 1) * tk]

    sbq_ref[0] = (_dot(h, wq_ref[...]) * (HEAD_DIM ** -0.5 * LOG2E)).astype(BF16)
    sbk_ref[0] = _dot(h, wk_ref[...]).astype(BF16)
    store_transposed(_dot(h, wv_ref[...]), sbv_ref)
    sbg_ref[0] = _dot(h, wg_ref[...])
    mg_ref[0] = _dot(h, wmg_ref[...])

    ang_t = freq_ref[...] * pos_ref[0]
    n_freq = ang_t.shape[0]
    frow = lax.broadcasted_iota(jnp.int32, (n_freq, LANES), 0)
    rope_lane = lax.broadcasted_iota(jnp.int32, (n_freq, LANES), 1) - QK_NOPE_DIM
    spread = ((rope_lane == frow) | (rope_lane - n_freq == frow)).astype(BF16)

    def on_lanes(t):
        hi = t.astype(BF16)
        lo = (t - hi.astype(F32)).astype(BF16)
        tn = lambda a: lax.dot_general(a, spread, (((0,), (0,)), ((), ())), preferred_element_type=F32)
        return tn(hi) + tn(lo)

    nope_lanes = (lax.broadcasted_iota(jnp.int32, (1, LANES), 1) < QK_NOPE_DIM).astype(F32)
    cos, sin = on_lanes(jnp.cos(ang_t)) + nope_lanes, on_lanes(jnp.sin(ang_t))

    cq = _rms(_dot(h, wcq_ref[...]), gq_ref[...]).astype(BF16)
    q2 = _dot(cq, wuq_ref[...])
    scale = (QK_NOPE_DIM + QK_ROPE_DIM) ** -0.5 * LOG2E
    cos_q, sin_q = cos * scale, sin * scale
    half = N_HEADS * LANES
    for hd in range(N_HEADS):
        qa = q2[:, hd * LANES:(hd + 1) * LANES]
        qb = q2[:, half + hd * LANES:half + (hd + 1) * LANES]
        mq_ref[0, :, hd * LANES:(hd + 1) * LANES] = (qa * cos_q + qb * sin_q).astype(BF16)

    ckv = _rms(_dot(h, wckv_ref[...]), gkv_ref[...]).astype(BF16)
    kv = _dot(ckv, wukv_ref[...])
    kr2 = _dot(h, wkr_ref[...])
    kr = kr2[:, :LANES] * cos + kr2[:, LANES:] * sin
    for hd in range(N_HEADS):
        mk_ref[0, hd] = (kv[:, hd * LANES:(hd + 1) * LANES] + kr).astype(BF16)
    store_transposed(kv[:, half:], mv_ref)


def _projections(x, pos, freq, gpre, wq, wk, wv, wg, wcq, wckv, wkr, wmg, gq, wuq, gkv, wukv):
    B, S, D = x.shape
    tm = min(PROJ_ROWS, S)
    assert S % tm == 0
    nt = S // tm
    full = lambda a: pl.BlockSpec(a.shape, lambda b, t: (0,) * a.ndim)
    row = lambda w: pl.BlockSpec((1, tm, w), lambda b, t: (b, t, 0))
    head = pl.BlockSpec((1, N_HEADS, tm, LANES), lambda b, t: (b, 0, t, 0))

    def vt(tile):
        tk = min(tile, S)
        assert tm % tk == 0
        return (pl.BlockSpec((1, N_HEADS, tm // tk, HEAD_DIM, tk), lambda b, t: (b, 0, t, 0, 0)),
                jax.ShapeDtypeStruct((B, N_HEADS, S // tk, HEAD_DIM, tk), BF16))

    (sb_vt, sb_vt_shape), (mla_vt, mla_vt_shape) = vt(SB_TK), vt(MLA_TK)
    weights = (freq, gpre, wq, wk, wv, wg, wcq, wckv, wkr, wmg, gq, wuq, gkv, wukv)
    return pl.pallas_call(
        _proj_kernel,
        grid=(B, nt),
        in_specs=[pl.BlockSpec((None, tm, D), lambda b, t: (b, t, 0)),
                  pl.BlockSpec((1, 1, tm), lambda b, t: (b, 0, t))] + [full(w) for w in weights],
        out_specs=[row(D_GROUP), row(D_GROUP), sb_vt, row(D_GROUP), row(N_HEADS * LANES), head, mla_vt, row(D_GROUP)],
        out_shape=[jax.ShapeDtypeStruct((B, S, D_GROUP), BF16), jax.ShapeDtypeStruct((B, S, D_GROUP), BF16),
                   sb_vt_shape,
                   jax.ShapeDtypeStruct((B, S, D_GROUP), F32),
                   jax.ShapeDtypeStruct((B, S, N_HEADS * LANES), BF16),
                   jax.ShapeDtypeStruct((B, N_HEADS, S, LANES), BF16), mla_vt_shape,
                   jax.ShapeDtypeStruct((B, S, D_GROUP), F32)],
        compiler_params=pltpu.CompilerParams(dimension_semantics=("arbitrary", "arbitrary"),
                                             vmem_limit_bytes=VMEM_LIMIT),
        name="proj_in",
    )(x, pos, *weights)


def _store_heads(o_t, g_ref, gate_ref, o_ref):
    for p in range(N_PAIRS):
        normed = [o * lax.rsqrt(jnp.mean(o * o, axis=0, keepdims=True) + EPS) for o in o_t[2 * p:2 * p + 2]]
        o_pair = jnp.concatenate(normed, axis=0).T
        cols = slice(p * LANES, (p + 1) * LANES)
        gate = gate_ref[0, :, cols]
        o_ref[0, :, cols] = (o_pair * g_ref[:, cols] * (gate * _sigmoid(gate))).astype(o_ref.dtype)


def _pipelined_heads(score, first, second):
    s = [score(h) for h in range(LOOKAHEAD)]
    mid = []
    for h in range(N_HEADS):
        mid.append(first(h, s[h]))
        if h + LOOKAHEAD < N_HEADS:
            s.append(score(h + LOOKAHEAD))
        if second is not None and h > 0:
            second(h - 1, s[h - 1], mid[h - 1])
    if second is not None:
        second(N_HEADS - 1, s[-1], mid[-1])


def _sb_kernel(q_ref, k_ref, v_ref, gate_ref, g_ref, o_ref, acc_ref, carry_ref):
    tq = q_ref.shape[1]
    tk = v_ref.shape[-1]
    ratio = tq // tk
    i = pl.program_id(1)
    low_half = lax.broadcasted_iota(jnp.int32, (tq, LANES), 1) < HEAD_DIM
    qs = []
    for p in range(N_PAIRS):
        qp = q_ref[0, :, p * LANES:(p + 1) * LANES]
        qs += [jnp.where(low_half, qp, 0), jnp.where(low_half, 0, qp)]
    key = lax.broadcasted_iota(jnp.int32, (tk, tq), 0)
    qry = lax.broadcasted_iota(jnp.int32, (tk, tq), 1)
    ksum = lax.broadcasted_iota(jnp.int32, (tk, 2 * tk), 1) % tk
    upper2 = (ksum >= lax.broadcasted_iota(jnp.int32, (tk, 2 * tk), 0)).astype(BF16)

    acc_ref[...] = jnp.zeros_like(acc_ref)
    carry_ref[...] = jnp.zeros_like(carry_ref)

    def step(j, past):
        start = pl.multiple_of(j * tk, tk)

        def score(h):
            pair = slice((h // 2) * LANES, (h // 2 + 1) * LANES)
            return _dot_nt(k_ref[0, pl.ds(start, tk), pair], qs[h])

        def suffix(h, z):
            sp = jnp.maximum(jnp.log2(1.0 + jnp.exp2(jnp.minimum(z, EXP2_CLAMP))), z)
            if past is not None:
                sp = jnp.where(past, sp, 0.0)
            hi = sp.astype(BF16)
            lo = (sp - hi.astype(F32)).astype(BF16)
            return _dot(upper2, jnp.concatenate([hi, lo], axis=0))

        def weigh(h, z, c):
            carry = carry_ref[h:h + 1, :]
            w = jnp.exp2(z - c - carry)
            if past is not None:
                w = jnp.where(past, w, 0.0)
            acc_ref[h] += _dot(v_ref[0, h, j], w.astype(BF16))
            carry_ref[h:h + 1, :] = carry + c[0:1, :]

        _pipelined_heads(score, suffix, weigh)

    for d in reversed(range(ratio)):
        step(i * ratio + d, (key + d * tk) < qry)

    n_off = i * ratio

    def body(state):
        n, _ = state
        step(n_off - 1 - n, None)
        return n + 1, jnp.min(carry_ref[...])

    lax.while_loop(lambda st: (st[0] < n_off) & (st[1] < SKIP_LOG2), body, (jnp.int32(0), jnp.float32(0.0)))
    _store_heads([acc_ref[h] for h in range(N_HEADS)], g_ref, gate_ref, o_ref)


def _mla_kernel(q_ref, k_ref, v_ref, gate_ref, g_ref, o_ref, acc_ref, m_ref):
    tq = q_ref.shape[1]
    tk = v_ref.shape[-1]
    ratio = tq // tk
    i = pl.program_id(1)
    key = lax.broadcasted_iota(jnp.int32, (tk, tq), 0)
    qry = lax.broadcasted_iota(jnp.int32, (tk, tq), 1)

    acc_ref[...] = jnp.zeros_like(acc_ref)
    m_ref[...] = jnp.full(m_ref.shape, -jnp.inf, F32)
    ones = jnp.ones((SUM_ROWS, tk), BF16)

    def step(j, visible):
        start = pl.multiple_of(j * tk, tk)

        def score(h):
            return _dot_nt(k_ref[0, h, pl.ds(start, tk), :], q_ref[0, :, h * LANES:(h + 1) * LANES])

        def update(h, s):
            if visible is not None:
                s = jnp.where(visible, s, -jnp.inf)
            m = m_ref[h:h + 1, :]
            m_new = jnp.maximum(m, jnp.max(s, axis=0, keepdims=True))
            alpha = jnp.exp2(m - m_new)
            p = jnp.exp2(s - m_new)
            m_ref[h:h + 1, :] = m_new
            v1 = jnp.concatenate([v_ref[0, h, j], ones], axis=0)
            acc_ref[h] = alpha * acc_ref[h] + _dot(v1, p.astype(BF16))

        _pipelined_heads(score, update, None)

    for d in range(ratio):
        step(i * ratio + d, ((key + d * tk) // CHUNK) <= (qry // CHUNK))

    def body(n, _):
        step(n, None)
        return 0

    lax.fori_loop(0, i * ratio, body, 0)
    _store_heads([acc_ref[h, :HEAD_DIM] / acc_ref[h, HEAD_DIM:HEAD_DIM + 1] for h in range(N_HEADS)],
                 g_ref, gate_ref, o_ref)


def _attention(body, name, tq, q, k, v, gate, g, k_spec, acc_rows):
    B, S, _ = gate.shape
    tq = min(tq, S)
    assert S % tq == 0 and tq % v.shape[-1] == 0
    return pl.pallas_call(
        body,
        grid=(B, S // tq),
        in_specs=[pl.BlockSpec((1, tq, q.shape[-1]), lambda b, i: (b, i, 0)),
                  k_spec,
                  pl.BlockSpec((1,) + v.shape[1:], lambda b, i: (b, 0, 0, 0, 0)),
                  pl.BlockSpec((1, tq, D_GROUP), lambda b, i: (b, i, 0)),
                  pl.BlockSpec((1, D_GROUP), lambda b, i: (0, 0))],
        out_specs=pl.BlockSpec((1, tq, D_GROUP), lambda b, i: (b, i, 0)),
        out_shape=jax.ShapeDtypeStruct((B, S, D_GROUP), BF16),
        scratch_shapes=[pltpu.VMEM((N_HEADS, acc_rows, tq), F32), pltpu.VMEM((N_HEADS, tq), F32)],
        compiler_params=pltpu.CompilerParams(dimension_semantics=("arbitrary", "arbitrary"),
                                             vmem_limit_bytes=VMEM_LIMIT),
        name=name,
    )(q, k, v, gate, g)


def _out_kernel(sb_ref, mla_ref, x_ref, p_ref, woa_ref, wob_ref, gpost_ref, wple_ref, gple_ref, wpg_ref, bpg_ref,
                o_ref):
    y = _dot(sb_ref[...], woa_ref[...]) + _dot(mla_ref[...], wob_ref[...])
    x1 = x_ref[...] + _rms(y, gpost_ref[...])
    ple = _rms(_dot(p_ref[...].astype(BF16), wple_ref[...]), gple_ref[...])
    gate = _sigmoid(_dot(x1.astype(BF16), wpg_ref[...]) + bpg_ref[...])
    o_ref[...] = x1 + ple * gate


def _output(sb_y, mla_y, x, p, woa, wob, gpost, wple, gple, wpg, bpg):
    B, S, D = x.shape
    tm = min(PROJ_ROWS, S)
    full = lambda a: pl.BlockSpec(a.shape, lambda b, t: (0,) * a.ndim)
    row = lambda w: pl.BlockSpec((None, tm, w), lambda b, t: (b, t, 0))
    weights = (woa, wob, gpost, wple, gple, wpg, bpg)
    return pl.pallas_call(
        _out_kernel,
        grid=(B, S // tm),
        in_specs=[row(D_GROUP), row(D_GROUP), row(D), row(p.shape[-1])] + [full(w) for w in weights],
        out_specs=row(D),
        out_shape=jax.ShapeDtypeStruct((B, S, D), F32),
        compiler_params=pltpu.CompilerParams(dimension_semantics=("arbitrary", "arbitrary"),
                                             vmem_limit_bytes=VMEM_LIMIT),
        name="proj_out",
    )(sb_y, mla_y, x, p, *weights)


def _rotate_half_cols(w):
    half = w.shape[-1] // 2
    return jnp.concatenate([-w[..., half:], w[..., :half]], axis=-1)


def _head_lanes(nope, rope):
    K = nope.shape[0]
    pad = jnp.zeros((K, N_HEADS, LANES - QK_NOPE_DIM - QK_ROPE_DIM), nope.dtype)
    return jnp.concatenate([nope, rope, pad], axis=-1).reshape(K, N_HEADS * LANES)


def _layer(x, p, pos, norm_pre_g, w_in, q_norm_g, w_uq, kv_norm_g, w_ukv, sb_out_norm_g, mla_out_norm_g, w_out,
           norm_post_g, w_ple, ple_norm_g, w_ple_gate, b_ple_gate):
    S = x.shape[1]
    row = lambda a: a.reshape(1, -1).astype(F32)
    c = np.cumsum([0, D_GROUP, D_GROUP, D_GROUP, D_GROUP, Q_LORA_RANK, KV_LORA_RANK, QK_ROPE_DIM, D_GROUP])
    wq, wk, wv, wg, wcq, wckv, wkr, wmg = (w_in[:, c[n]:c[n + 1]] for n in range(8))

    def rope_lanes(w):
        return jnp.pad(w, ((0, 0), (QK_NOPE_DIM, LANES - QK_NOPE_DIM - QK_ROPE_DIM)))

    wkr2 = jnp.concatenate([rope_lanes(wkr), rope_lanes(_rotate_half_cols(wkr))], axis=-1)
    uq = w_uq.reshape(Q_LORA_RANK, N_HEADS, QK_NOPE_DIM + QK_ROPE_DIM)
    uq_nope, uq_rope = uq[..., :QK_NOPE_DIM], uq[..., QK_NOPE_DIM:]
    wuq2 = jnp.concatenate([_head_lanes(uq_nope, uq_rope),
                            _head_lanes(jnp.zeros_like(uq_nope), _rotate_half_cols(uq_rope))], axis=-1)
    ukv = w_ukv.reshape(KV_LORA_RANK, N_HEADS, QK_NOPE_DIM + HEAD_DIM)
    uk, uv = ukv[..., :QK_NOPE_DIM], ukv[..., QK_NOPE_DIM:]
    wukv2 = jnp.concatenate([_head_lanes(uk, jnp.zeros((KV_LORA_RANK, N_HEADS, QK_ROPE_DIM), uk.dtype)),
                             uv.reshape(KV_LORA_RANK, D_GROUP)], axis=-1)
    half = QK_ROPE_DIM // 2
    freq = (ROPE_THETA ** (-jnp.arange(half, dtype=F32) / half)).reshape(half, 1)
    bf = lambda a: a.astype(BF16)

    sbq, sbk, sbv, sbg, mq, mk, mv, mg = _projections(
        x, pos.astype(F32)[:, None, :], freq, row(norm_pre_g), bf(wq), bf(wk), bf(wv), bf(wg), bf(wcq), bf(wckv),
        bf(wkr2), bf(wmg), row(q_norm_g), bf(wuq2), row(kv_norm_g), bf(wukv2))
    sb_k_spec = pl.BlockSpec((1, S, D_GROUP), lambda b, i: (b, 0, 0))
    mla_k_spec = pl.BlockSpec((1, N_HEADS, S, LANES), lambda b, i: (b, 0, 0, 0))
    sb_y = _attention(_sb_kernel, "sb_attn", SB_TQ, sbq, sbk, sbv, sbg, row(sb_out_norm_g), sb_k_spec, HEAD_DIM)
    mla_y = _attention(_mla_kernel, "mla_attn", MLA_TQ, mq, mk, mv, mg, row(mla_out_norm_g), mla_k_spec,
                       HEAD_DIM + SUM_ROWS)
    return _output(sb_y, mla_y, x, p, bf(w_out[:D_GROUP]), bf(w_out[D_GROUP:]), row(norm_post_g), bf(w_ple),
                   row(ple_norm_g), bf(w_ple_gate), row(b_ple_gate))


def kernel(x, p, positions, norm_pre_g, w_in, q_norm_g, w_uq, kv_norm_g, w_ukv, sb_out_norm_g, mla_out_norm_g, w_out,
           norm_post_g, w_ple, ple_norm_g, w_ple_gate, b_ple_gate):
    for i in range(p.shape[0]):
        x = _layer(x, p[i], positions, norm_pre_g[i], w_in[i], q_norm_g[i], w_uq[i], kv_norm_g[i], w_ukv[i],
                   sb_out_norm_g[i], mla_out_norm_g[i], w_out[i], norm_post_g[i], w_ple[i], ple_norm_g[i],
                   w_ple_gate[i], b_ple_gate[i])
    return x
```

```python
import jax
import jax.numpy as jnp
import numpy as np
from jax import lax
from jax.experimental import pallas as pl
from jax.experimental.pallas import tpu as pltpu

HEAD_DIM = 64
N_HEADS = 8
N_PAIRS = N_HEADS // 2
LANES = 128
QK_NOPE_DIM = 64
QK_ROPE_DIM = 32
Q_LORA_RANK = 256
KV_LORA_RANK = 128
CHUNK = 64
ROPE_THETA = 10000.0
EPS = 1e-6
D_GROUP = N_HEADS * HEAD_DIM
LOG2E = 1.4426950408889634

PROJ_ROWS = 512
SB_TK = 256
SB_TQ = 256
MLA_TK = 512
MLA_TQ = 512
SUM_ROWS = 16
LOOKAHEAD = 3
SKIP_LOG2 = 152.0
EXP2_CLAMP = 126.0
VMEM_LIMIT = 56 * 1024 * 1024

F32 = jnp.float32
BF16 = jnp.bfloat16


def _rms(x, g):
    return x * lax.rsqrt(jnp.mean(x * x, axis=-1, keepdims=True) + EPS) * g


def _dot(a, b):
    return jnp.dot(a, b, preferred_element_type=F32)


def _dot_nt(a, b):
    return lax.dot_general(a, b, (((1,), (1,)), ((), ())), preferred_element_type=F32)


def _sigmoid(x):
    return 1.0 / (1.0 + jnp.exp(-x))


def _proj_kernel(x_ref, pos_ref, freq_ref, gpre_ref, wq_ref, wk_ref, wv_ref, wg_ref, wcq_ref, wckv_ref,
                 wkr_ref, wmg_ref, gq_ref, wuq_ref, gkv_ref, wukv_ref,
                 sbq_ref, sbk_ref, sbv_ref, sbg_ref, mq_ref, mk_ref, mv_ref, mg_ref):
    h = _rms(x_ref[...], gpre_ref[...]).astype(BF16)

    def store_transposed(a, out_ref):
        tk = out_ref.shape[-1]
        for p in range(N_PAIRS):
            at = a[:, p * LANES:(p + 1) * LANES].T.astype(BF16)
            for c in range(out_ref.shape[2]):
                out_ref[0, 2 * p, c] = at[:HEAD_DIM, c * tk:(c + 1) * tk]
                out_ref[0, 2 * p + 1, c] = at[HEAD_DIM:, c * tk:(c + 1) * tk]

    sbq_ref[0] = (_dot(h, wq_ref[...]) * (HEAD_DIM ** -0.5 * LOG2E)).astype(BF16)
    sbk_ref[0] = _dot(h, wk_ref[...]).astype(BF16)
    store_transposed(_dot(h, wv_ref[...]), sbv_ref)
    sbg_ref[0] = _dot(h, wg_ref[...])
    mg_ref[0] = _dot(h, wmg_ref[...])

    ang_t = freq_ref[...] * pos_ref[0]
    n_freq = ang_t.shape[0]
    frow = lax.broadcasted_iota(jnp.int32, (n_freq, LANES), 0)
    rope_lane = lax.broadcasted_iota(jnp.int32, (n_freq, LANES), 1) - QK_NOPE_DIM
    spread = ((rope_lane == frow) | (rope_lane - n_freq == frow)).astype(BF16)

    def on_lanes(t):
        hi = t.astype(BF16)
        lo = (t - hi.astype(F32)).astype(BF16)
        tn = lambda a: lax.dot_general(a, spread, (((0,), (0,)), ((), ())), preferred_element_type=F32)
        return tn(hi) + tn(lo)

    nope_lanes = (lax.broadcasted_iota(jnp.int32, (1, LANES), 1) < QK_NOPE_DIM).astype(F32)
    cos, sin = on_lanes(jnp.cos(ang_t)) + nope_lanes, on_lanes(jnp.sin(ang_t))

    cq = _rms(_dot(h, wcq_ref[...]), gq_ref[...]).astype(BF16)
    q2 = _dot(cq, wuq_ref[...])
    scale = (QK_NOPE_DIM + QK_ROPE_DIM) ** -0.5 * LOG2E
    cos_q, sin_q = cos * scale, sin * scale
    half = N_HEADS * LANES
    for hd in range(N_HEADS):
        qa = q2[:, hd * LANES:(hd + 1) * LANES]
        qb = q2[:, half + hd * LANES:half + (hd + 1) * LANES]
        mq_ref[0, :, hd * LANES:(hd + 1) * LANES] = (qa * cos_q + qb * sin_q).astype(BF16)

    ckv = _rms(_dot(h, wckv_ref[...]), gkv_ref[...]).astype(BF16)
    kv = _dot(ckv, wukv_ref[...])
    kr2 = _dot(h, wkr_ref[...])
    kr = kr2[:, :LANES] * cos + kr2[:, LANES:] * sin
    for hd in range(N_HEADS):
        mk_ref[0, hd] = (kv[:, hd * LANES:(hd + 1) * LANES] + kr).astype(BF16)
    store_transposed(kv[:, half:], mv_ref)


def _projections(x, pos, freq, gpre, wq, wk, wv, wg, wcq, wckv, wkr, wmg, gq, wuq, gkv, wukv):
    B, S, D = x.shape
    tm = min(PROJ_ROWS, S)
    assert S % tm == 0
    nt = S // tm
    full = lambda a: pl.BlockSpec(a.shape, lambda b, t: (0,) * a.ndim)
    row = lambda w: pl.BlockSpec((1, tm, w), lambda b, t: (b, t, 0))
    head = pl.BlockSpec((1, N_HEADS, tm, LANES), lambda b, t: (b, 0, t, 0))

    def vt(tile):
        tk = min(tile, S)
        assert tm % tk == 0
        return (pl.BlockSpec((1, N_HEADS, tm // tk, HEAD_DIM, tk), lambda b, t: (b, 0, t, 0, 0)),
                jax.ShapeDtypeStruct((B, N_HEADS, S // tk, HEAD_DIM, tk), BF16))

    (sb_vt, sb_vt_shape), (mla_vt, mla_vt_shape) = vt(SB_TK), vt(MLA_TK)
    weights = (freq, gpre, wq, wk, wv, wg, wcq, wckv, wkr, wmg, gq, wuq, gkv, wukv)
    return pl.pallas_call(
        _proj_kernel,
        grid=(B, nt),
        in_specs=[pl.BlockSpec((None, tm, D), lambda b, t: (b, t, 0)),
                  pl.BlockSpec((1, 1, tm), lambda b, t: (b, 0, t))] + [full(w) for w in weights],
        out_specs=[row(D_GROUP), row(D_GROUP), sb_vt, row(D_GROUP), row(N_HEADS * LANES), head, mla_vt, row(D_GROUP)],
        out_shape=[jax.ShapeDtypeStruct((B, S, D_GROUP), BF16), jax.ShapeDtypeStruct((B, S, D_GROUP), BF16),
                   sb_vt_shape,
                   jax.ShapeDtypeStruct((B, S, D_GROUP), F32),
                   jax.ShapeDtypeStruct((B, S, N_HEADS * LANES), BF16),
                   jax.ShapeDtypeStruct((B, N_HEADS, S, LANES), BF16), mla_vt_shape,
                   jax.ShapeDtypeStruct((B, S, D_GROUP), F32)],
        compiler_params=pltpu.CompilerParams(dimension_semantics=("arbitrary", "arbitrary"),
                                             vmem_limit_bytes=VMEM_LIMIT),
        name="proj_in",
    )(x, pos, *weights)


def _store_heads(o_t, g_ref, gate_ref, o_ref):
    for p in range(N_PAIRS):
        normed = [o * lax.rsqrt(jnp.mean(o * o, axis=0, keepdims=True) + EPS) for o in o_t[2 * p:2 * p + 2]]
        o_pair = jnp.concatenate(normed, axis=0).T
        cols = slice(p * LANES, (p + 1) * LANES)
        gate = gate_ref[0, :, cols]
        o_ref[0, :, cols] = (o_pair * g_ref[:, cols] * (gate * _sigmoid(gate))).astype(o_ref.dtype)


def _pipelined_heads(score, first, second):
    s = [score(h) for h in range(LOOKAHEAD)]
    mid = []
    for h in range(N_HEADS):
        mid.append(first(h, s[h]))
        if h + LOOKAHEAD < N_HEADS:
            s.append(score(h + LOOKAHEAD))
        if second is not None and h > 0:
            second(h - 1, s[h - 1], mid[h - 1])
    if second is not None:
        second(N_HEADS - 1, s[-1], mid[-1])


def _sb_kernel(q_ref, k_ref, v_ref, gate_ref, g_ref, o_ref, acc_ref, carry_ref):
    tq = q_ref.shape[1]
    tk = v_ref.shape[-1]
    ratio = tq // tk
    i = pl.program_id(1)
    low_half = lax.broadcasted_iota(jnp.int32, (tq, LANES), 1) < HEAD_DIM
    qs = []
    for p in range(N_PAIRS):
        qp = q_ref[0, :, p * LANES:(p + 1) * LANES]
        qs += [jnp.where(low_half, qp, 0), jnp.where(low_half, 0, qp)]
    key = lax.broadcasted_iota(jnp.int32, (tk, tq), 0)
    qry = lax.broadcasted_iota(jnp.int32, (tk, tq), 1)
    upper = (lax.broadcasted_iota(jnp.int32, (tk, tk), 1) >= lax.broadcasted_iota(jnp.int32, (tk, tk), 0)).astype(BF16)

    acc_ref[...] = jnp.zeros_like(acc_ref)
    carry_ref[...] = jnp.zeros_like(carry_ref)

    def step(j, past):
        start = pl.multiple_of(j * tk, tk)

        def score(h):
            pair = slice((h // 2) * LANES, (h // 2 + 1) * LANES)
            z = _dot_nt(k_ref[0, pl.ds(start, tk), pair], qs[h])
            return z if past is None else jnp.where(past, z, -jnp.inf)

        def suffix(h, z):
            sp = jnp.maximum(jnp.log2(1.0 + jnp.exp2(jnp.minimum(z, EXP2_CLAMP))), z)
            return _dot(upper, sp.astype(BF16))

        def weigh(h, z, c):
            carry = carry_ref[h:h + 1, :]
            w = jnp.exp2(z - c - carry)
            acc_ref[h] += _dot(v_ref[0, h, j], w.astype(BF16))
            carry_ref[h:h + 1, :] = carry + c[0:1, :]

        _pipelined_heads(score, suffix, weigh)

    for d in reversed(range(ratio)):
        step(i * ratio + d, (key + d * tk) < qry)

    n_off = i * ratio

    def body(state):
        n, _ = state
        step(n_off - 1 - n, None)
        return n + 1, jnp.min(carry_ref[...])

    lax.while_loop(lambda st: (st[0] < n_off) & (st[1] < SKIP_LOG2), body, (jnp.int32(0), jnp.float32(0.0)))
    _store_heads([acc_ref[h] for h in range(N_HEADS)], g_ref, gate_ref, o_ref)


def _mla_kernel(q_ref, k_ref, v_ref, gate_ref, g_ref, o_ref, acc_ref, m_ref):
    tq = q_ref.shape[1]
    tk = v_ref.shape[-1]
    ratio = tq // tk
    i = pl.program_id(1)
    key = lax.broadcasted_iota(jnp.int32, (tk, tq), 0)
    qry = lax.broadcasted_iota(jnp.int32, (tk, tq), 1)

    acc_ref[...] = jnp.zeros_like(acc_ref)
    m_ref[...] = jnp.full(m_ref.shape, -jnp.inf, F32)
    ones = jnp.ones((SUM_ROWS, tk), BF16)

    def step(j, visible):
        start = pl.multiple_of(j * tk, tk)

        def score(h):
            return _dot_nt(k_ref[0, h, pl.ds(start, tk), :], q_ref[0, :, h * LANES:(h + 1) * LANES])

        def update(h, s):
            if visible is not None:
                s = jnp.where(visible, s, -jnp.inf)
            m = m_ref[h:h + 1, :]
            m_new = jnp.maximum(m, jnp.max(s, axis=0, keepdims=True))
            alpha = jnp.exp2(m - m_new)
            p = jnp.exp2(s - m_new)
            m_ref[h:h + 1, :] = m_new
            v1 = jnp.concatenate([v_ref[0, h, j], ones], axis=0)
            acc_ref[h] = alpha * acc_ref[h] + _dot(v1, p.astype(BF16))

        _pipelined_heads(score, update, None)

    for d in range(ratio):
        step(i * ratio + d, ((key + d * tk) // CHUNK) <= (qry // CHUNK))

    def body(n, _):
        step(n, None)
        return 0

    lax.fori_loop(0, i * ratio, body, 0)
    _store_heads([acc_ref[h, :HEAD_DIM] / acc_ref[h, HEAD_DIM:HEAD_DIM + 1] for h in range(N_HEADS)],
                 g_ref, gate_ref, o_ref)


def _attention(body, name, tq, q, k, v, gate, g, k_spec, acc_rows):
    B, S, _ = gate.shape
    tq = min(tq, S)
    assert S % tq == 0 and tq % v.shape[-1] == 0
    return pl.pallas_call(
        body,
        grid=(B, S // tq),
        in_specs=[pl.BlockSpec((1, tq, q.shape[-1]), lambda b, i: (b, i, 0)),
                  k_spec,
                  pl.BlockSpec((1,) + v.shape[1:], lambda b, i: (b, 0, 0, 0, 0)),
                  pl.BlockSpec((1, tq, D_GROUP), lambda b, i: (b, i, 0)),
                  pl.BlockSpec((1, D_GROUP), lambda b, i: (0, 0))],
        out_specs=pl.BlockSpec((1, tq, D_GROUP), lambda b, i: (b, i, 0)),
        out_shape=jax.ShapeDtypeStruct((B, S, D_GROUP), BF16),
        scratch_shapes=[pltpu.VMEM((N_HEADS, acc_rows, tq), F32), pltpu.VMEM((N_HEADS, tq), F32)],
        compiler_params=pltpu.CompilerParams(dimension_semantics=("arbitrary", "arbitrary"),
                                             vmem_limit_bytes=VMEM_LIMIT),
        name=name,
    )(q, k, v, gate, g)


def _out_kernel(sb_ref, mla_ref, x_ref, p_ref, woa_ref, wob_ref, gpost_ref, wple_ref, gple_ref, wpg_ref, bpg_ref,
                o_ref):
    y = _dot(sb_ref[...], woa_ref[...]) + _dot(mla_ref[...], wob_ref[...])
    x1 = x_ref[...] + _rms(y, gpost_ref[...])
    ple = _rms(_dot(p_ref[...].astype(BF16), wple_ref[...]), gple_ref[...])
    gate = _sigmoid(_dot(x1.astype(BF16), wpg_ref[...]) + bpg_ref[...])
    o_ref[...] = x1 + ple * gate


def _output(sb_y, mla_y, x, p, woa, wob, gpost, wple, gple, wpg, bpg):
    B, S, D = x.shape
    tm = min(PROJ_ROWS, S)
    full = lambda a: pl.BlockSpec(a.shape, lambda b, t: (0,) * a.ndim)
    row = lambda w: pl.BlockSpec((None, tm, w), lambda b, t: (b, t, 0))
    weights = (woa, wob, gpost, wple, gple, wpg, bpg)
    return pl.pallas_call(
        _out_kernel,
        grid=(B, S // tm),
        in_specs=[row(D_GROUP), row(D_GROUP), row(D), row(p.shape[-1])] + [full(w) for w in weights],
        out_specs=row(D),
        out_shape=jax.ShapeDtypeStruct((B, S, D), F32),
        compiler_params=pltpu.CompilerParams(dimension_semantics=("arbitrary", "arbitrary"),
                                             vmem_limit_bytes=VMEM_LIMIT),
        name="proj_out",
    )(sb_y, mla_y, x, p, *weights)


def _rotate_half_cols(w):
    half = w.shape[-1] // 2
    return jnp.concatenate([-w[..., half:], w[..., :half]], axis=-1)


def _head_lanes(nope, rope):
    K = nope.shape[0]
    pad = jnp.zeros((K, N_HEADS, LANES - QK_NOPE_DIM - QK_ROPE_DIM), nope.dtype)
    return jnp.concatenate([nope, rope, pad], axis=-1).reshape(K, N_HEADS * LANES)


def _layer(x, p, pos, norm_pre_g, w_in, q_norm_g, w_uq, kv_norm_g, w_ukv, sb_out_norm_g, mla_out_norm_g, w_out,
           norm_post_g, w_ple, ple_norm_g, w_ple_gate, b_ple_gate):
    S = x.shape[1]
    row = lambda a: a.reshape(1, -1).astype(F32)
    c = np.cumsum([0, D_GROUP, D_GROUP, D_GROUP, D_GROUP, Q_LORA_RANK, KV_LORA_RANK, QK_ROPE_DIM, D_GROUP])
    wq, wk, wv, wg, wcq, wckv, wkr, wmg = (w_in[:, c[n]:c[n + 1]] for n in range(8))

    def rope_lanes(w):
        return jnp.pad(w, ((0, 0), (QK_NOPE_DIM, LANES - QK_NOPE_DIM - QK_ROPE_DIM)))

    wkr2 = jnp.concatenate([rope_lanes(wkr), rope_lanes(_rotate_half_cols(wkr))], axis=-1)
    uq = w_uq.reshape(Q_LORA_RANK, N_HEADS, QK_NOPE_DIM + QK_ROPE_DIM)
    uq_nope, uq_rope = uq[..., :QK_NOPE_DIM], uq[..., QK_NOPE_DIM:]
    wuq2 = jnp.concatenate([_head_lanes(uq_nope, uq_rope),
                            _head_lanes(jnp.zeros_like(uq_nope), _rotate_half_cols(uq_rope))], axis=-1)
    ukv = w_ukv.reshape(KV_LORA_RANK, N_HEADS, QK_NOPE_DIM + HEAD_DIM)
    uk, uv = ukv[..., :QK_NOPE_DIM], ukv[..., QK_NOPE_DIM:]
    wukv2 = jnp.concatenate([_head_lanes(uk, jnp.zeros((KV_LORA_RANK, N_HEADS, QK_ROPE_DIM), uk.dtype)),
                             uv.reshape(KV_LORA_RANK, D_GROUP)], axis=-1)
    half = QK_ROPE_DIM // 2
    freq = (ROPE_THETA ** (-jnp.arange(half, dtype=F32) / half)).reshape(half, 1)
    bf = lambda a: a.astype(BF16)

    sbq, sbk, sbv, sbg, mq, mk, mv, mg = _projections(
        x, pos.astype(F32)[:, None, :], freq, row(norm_pre_g), bf(wq), bf(wk), bf(wv), bf(wg), bf(wcq), bf(wckv),
        bf(wkr2), bf(wmg), row(q_norm_g), bf(wuq2), row(kv_norm_g), bf(wukv2))
    sb_k_spec = pl.BlockSpec((1, S, D_GROUP), lambda b, i: (b, 0, 0))
    mla_k_spec = pl.BlockSpec((1, N_HEADS, S, LANES), lambda b, i: (b, 0, 0, 0))
    sb_y = _attention(_sb_kernel, "sb_attn", SB_TQ, sbq, sbk, sbv, sbg, row(sb_out_norm_g), sb_k_spec, HEAD_DIM)
    mla_y = _attention(_mla_kernel, "mla_attn", MLA_TQ, mq, mk, mv, mg, row(mla_out_norm_g), mla_k_spec,
                       HEAD_DIM + SUM_ROWS)
    return _output(sb_y, mla_y, x, p, bf(w_out[:D_GROUP]), bf(w_out[D_GROUP:]), row(norm_post_g), bf(w_ple),
                   row(ple_norm_g), bf(w_ple_gate), row(b_ple_gate))


def kernel(x, p, positions, norm_pre_g, w_in, q_norm_g, w_uq, kv_norm_g, w_ukv, sb_out_norm_g, mla_out_norm_g, w_out,
           norm_post_g, w_ple, ple_norm_g, w_ple_gate, b_ple_gate):
    for i in range(p.shape[0]):
        x = _layer(x, p[i], positions, norm_pre_g[i], w_in[i], q_norm_g[i], w_uq[i], kv_norm_g[i], w_ukv[i],
                   sb_out_norm_g[i], mla_out_norm_g[i], w_out[i], norm_post_g[i], w_ple[i], ple_norm_g[i],
                   w_ple_gate[i], b_ple_gate[i])
    return x
```

```python
import jax
import jax.numpy as jnp
import numpy as np
from jax import lax
from jax.experimental import pallas as pl
from jax.experimental.pallas import tpu as pltpu

HEAD_DIM = 64
N_HEADS = 8
N_PAIRS = N_HEADS // 2
LANES = 128
QK_NOPE_DIM = 64
QK_ROPE_DIM = 32
Q_LORA_RANK = 256
KV_LORA_RANK = 128
CHUNK = 64
ROPE_THETA = 10000.0
EPS = 1e-6
D_GROUP = N_HEADS * HEAD_DIM
LOG2E = 1.4426950408889634

PROJ_ROWS = 512
SB_TK = 256
SB_TQ = 256
MLA_TK = 512
MLA_TQ = 512
SUM_ROWS = 16
LOOKAHEAD = 3
SKIP_LOG2 = 152.0
EXP2_CLAMP = 126.0
VMEM_LIMIT = 56 * 1024 * 1024

F32 = jnp.float32
BF16 = jnp.bfloat16


def _rms(x, g):
    return x * lax.rsqrt(jnp.mean(x * x, axis=-1, keepdims=True) + EPS) * g


def _dot(a, b):
    return jnp.dot(a, b, preferred_element_type=F32)


def _dot_nt(a, b):
    return lax.dot_general(a, b, (((1,), (1,)), ((), ())), preferred_element_type=F32)


def _sigmoid(x):
    return 1.0 / (1.0 + jnp.exp(-x))


def _proj_kernel(x_ref, pos_ref, freq_ref, gpre_ref, wq_ref, wk_ref, wv_ref, wg_ref, wcq_ref, wckv_ref,
                 wkr_ref, wmg_ref, gq_ref, wuq_ref, gkv_ref, wukv_ref,
                 sbq_ref, sbk_ref, sbv_ref, sbg_ref, mq_ref, mk_ref, mv_ref, mg_ref):
    h = _rms(x_ref[...], gpre_ref[...]).astype(BF16)

    def store_transposed(a, out_ref):
        tk = out_ref.shape[-1]
        for p in range(N_PAIRS):
            at = a[:, p * LANES:(p + 1) * LANES].T.astype(BF16)
            for c in range(out_ref.shape[2]):
                out_ref[0, 2 * p, c] = at[:HEAD_DIM, c * tk:(c + 1) * tk]
                out_ref[0, 2 * p + 1, c] = at[HEAD_DIM:, c * tk:(c + 1) * tk]

    sbq_ref[0] = (_dot(h, wq_ref[...]) * (HEAD_DIM ** -0.5 * LOG2E)).astype(BF16)
    sbk_ref[0] = _dot(h, wk_ref[...]).astype(BF16)
    store_transposed(_dot(h, wv_ref[...]), sbv_ref)
    sbg_ref[0] = _dot(h, wg_ref[...])
    mg_ref[0] = _dot(h, wmg_ref[...])

    ang_t = freq_ref[...] * pos_ref[0]
    n_freq = ang_t.shape[0]
    frow = lax.broadcasted_iota(jnp.int32, (n_freq, LANES), 0)
    rope_lane = lax.broadcasted_iota(jnp.int32, (n_freq, LANES), 1) - QK_NOPE_DIM
    spread = ((rope_lane == frow) | (rope_lane - n_freq == frow)).astype(BF16)

    def on_lanes(t):
        hi = t.astype(BF16)
        lo = (t - hi.astype(F32)).astype(BF16)
        tn = lambda a: lax.dot_general(a, spread, (((0,), (0,)), ((), ())), preferred_element_type=F32)
        return tn(hi) + tn(lo)

    nope_lanes = (lax.broadcasted_iota(jnp.int32, (1, LANES), 1) < QK_NOPE_DIM).astype(F32)
    cos, sin = on_lanes(jnp.cos(ang_t)) + nope_lanes, on_lanes(jnp.sin(ang_t))

    cq = _rms(_dot(h, wcq_ref[...]), gq_ref[...]).astype(BF16)
    q2 = _dot(cq, wuq_ref[...])
    scale = (QK_NOPE_DIM + QK_ROPE_DIM) ** -0.5 * LOG2E
    cos_q, sin_q = cos * scale, sin * scale
    half = N_HEADS * LANES
    for hd in range(N_HEADS):
        qa = q2[:, hd * LANES:(hd + 1) * LANES]
        qb = q2[:, half + hd * LANES:half + (hd + 1) * LANES]
        mq_ref[0, :, hd * LANES:(hd + 1) * LANES] = (qa * cos_q + qb * sin_q).astype(BF16)

    ckv = _rms(_dot(h, wckv_ref[...]), gkv_ref[...]).astype(BF16)
    kv = _dot(ckv, wukv_ref[...])
    kr2 = _dot(h, wkr_ref[...])
    kr = kr2[:, :LANES] * cos + kr2[:, LANES:] * sin
    for hd in range(N_HEADS):
        mk_ref[0, hd] = (kv[:, hd * LANES:(hd + 1) * LANES] + kr).astype(BF16)
    store_transposed(kv[:, half:], mv_ref)


def _projections(x, pos, freq, gpre, wq, wk, wv, wg, wcq, wckv, wkr, wmg, gq, wuq, gkv, wukv):
    B, S, D = x.shape
    tm = min(PROJ_ROWS, S)
    assert S % tm == 0
    nt = S // tm
    full = lambda a: pl.BlockSpec(a.shape, lambda b, t: (0,) * a.ndim)
    row = lambda w: pl.BlockSpec((1, tm, w), lambda b, t: (b, t, 0))
    head = pl.BlockSpec((1, N_HEADS, tm, LANES), lambda b, t: (b, 0, t, 0))

    def vt(tile):
        tk = min(tile, S)
        assert tm % tk == 0
        return (pl.BlockSpec((1, N_HEADS, tm // tk, HEAD_DIM, tk), lambda b, t: (b, 0, t, 0, 0)),
                jax.ShapeDtypeStruct((B, N_HEADS, S // tk, HEAD_DIM, tk), BF16))

    (sb_vt, sb_vt_shape), (mla_vt, mla_vt_shape) = vt(SB_TK), vt(MLA_TK)
    weights = (freq, gpre, wq, wk, wv, wg, wcq, wckv, wkr, wmg, gq, wuq, gkv, wukv)
    return pl.pallas_call(
        _proj_kernel,
        grid=(B, nt),
        in_specs=[pl.BlockSpec((None, tm, D), lambda b, t: (b, t, 0)),
                  pl.BlockSpec((1, 1, tm), lambda b, t: (b, 0, t))] + [full(w) for w in weights],
        out_specs=[row(D_GROUP), row(D_GROUP), sb_vt, row(D_GROUP), row(N_HEADS * LANES), head, mla_vt, row(D_GROUP)],
        out_shape=[jax.ShapeDtypeStruct((B, S, D_GROUP), BF16), jax.ShapeDtypeStruct((B, S, D_GROUP), BF16),
                   sb_vt_shape,
                   jax.ShapeDtypeStruct((B, S, D_GROUP), F32),
                   jax.ShapeDtypeStruct((B, S, N_HEADS * LANES), BF16),
                   jax.ShapeDtypeStruct((B, N_HEADS, S, LANES), BF16), mla_vt_shape,
                   jax.ShapeDtypeStruct((B, S, D_GROUP), F32)],
        compiler_params=pltpu.CompilerParams(dimension_semantics=("arbitrary", "arbitrary"),
                                             vmem_limit_bytes=VMEM_LIMIT),
        name="proj_in",
    )(x, pos, *weights)


def _store_heads(o_t, g_ref, gate_ref, o_ref):
    for p in range(N_PAIRS):
        normed = [o * lax.rsqrt(jnp.mean(o * o, axis=0, keepdims=True) + EPS) for o in o_t[2 * p:2 * p + 2]]
        o_pair = jnp.concatenate(normed, axis=0).T
        cols = slice(p * LANES, (p + 1) * LANES)
        gate = gate_ref[0, :, cols]
        o_ref[0, :, cols] = (o_pair * g_ref[:, cols] * (gate * _sigmoid(gate))).astype(o_ref.dtype)


def _pipelined_heads(score, first, second, stash_ref=None, stashed=False, score_next=None):
    s = [stash_ref[h] if stashed else score(h) for h in range(LOOKAHEAD)]
    mid = []
    for h in range(N_HEADS):
        mid.append(first(h, s[h]))
        if h + LOOKAHEAD < N_HEADS:
            s.append(score(h + LOOKAHEAD))
        elif stash_ref is not None:
            stash_ref[h + LOOKAHEAD - N_HEADS] = score_next(h + LOOKAHEAD - N_HEADS)
        if second is not None and h > 0:
            second(h - 1, s[h - 1], mid[h - 1])
    if second is not None:
        second(N_HEADS - 1, s[-1], mid[-1])


def _sb_kernel(q_ref, k_ref, v_ref, gate_ref, g_ref, o_ref, acc_ref, carry_ref, stash_ref):
    tq = q_ref.shape[1]
    tk = v_ref.shape[-1]
    ratio = tq // tk
    i = pl.program_id(1)
    low_half = lax.broadcasted_iota(jnp.int32, (tq, LANES), 1) < HEAD_DIM
    qs = []
    for p in range(N_PAIRS):
        qp = q_ref[0, :, p * LANES:(p + 1) * LANES]
        qs += [jnp.where(low_half, qp, 0), jnp.where(low_half, 0, qp)]
    key = lax.broadcasted_iota(jnp.int32, (tk, tq), 0)
    qry = lax.broadcasted_iota(jnp.int32, (tk, tq), 1)
    upper = (lax.broadcasted_iota(jnp.int32, (tk, tk), 1) >= lax.broadcasted_iota(jnp.int32, (tk, tk), 0)).astype(BF16)

    acc_ref[...] = jnp.zeros_like(acc_ref)
    carry_ref[...] = jnp.zeros_like(carry_ref)

    def step(j, j_next, past, stashed):
        def raw_score(h, tile):
            pair = slice((h // 2) * LANES, (h // 2 + 1) * LANES)
            return _dot_nt(k_ref[0, pl.ds(pl.multiple_of(tile * tk, tk), tk), pair], qs[h])

        def score(h):
            z = raw_score(h, j)
            return z if past is None else jnp.where(past, z, -jnp.inf)

        def suffix(h, z):
            sp = jnp.maximum(jnp.log2(1.0 + jnp.exp2(jnp.minimum(z, EXP2_CLAMP))), z)
            return _dot(upper, sp.astype(BF16))

        def weigh(h, z, c):
            carry = carry_ref[h:h + 1, :]
            w = jnp.exp2(z - c - carry)
            acc_ref[h] += _dot(v_ref[0, h, j], w.astype(BF16))
            carry_ref[h:h + 1, :] = carry + c[0:1, :]

        _pipelined_heads(score, suffix, weigh, stash_ref, stashed, lambda h: raw_score(h, j_next))

    for d in reversed(range(ratio)):
        step(i * ratio + d, jnp.maximum(i * ratio + d - 1, 0), (key + d * tk) < qry, False)

    n_off = i * ratio

    def body(state):
        n, _ = state
        step(n_off - 1 - n, jnp.maximum(n_off - 2 - n, 0), None, True)
        return n + 1, jnp.min(carry_ref[...])

    lax.while_loop(lambda st: (st[0] < n_off) & (st[1] < SKIP_LOG2), body, (jnp.int32(0), jnp.float32(0.0)))
    _store_heads([acc_ref[h] for h in range(N_HEADS)], g_ref, gate_ref, o_ref)


def _mla_kernel(q_ref, k_ref, v_ref, gate_ref, g_ref, o_ref, acc_ref, m_ref):
    tq = q_ref.shape[1]
    tk = v_ref.shape[-1]
    ratio = tq // tk
    i = pl.program_id(1)
    key = lax.broadcasted_iota(jnp.int32, (tk, tq), 0)
    qry = lax.broadcasted_iota(jnp.int32, (tk, tq), 1)

    acc_ref[...] = jnp.zeros_like(acc_ref)
    m_ref[...] = jnp.full(m_ref.shape, -jnp.inf, F32)
    ones = jnp.ones((SUM_ROWS, tk), BF16)

    def step(j, visible):
        def score(h):
            rows = pl.ds(pl.multiple_of(j * tk, tk), tk)
            return _dot_nt(k_ref[0, h, rows, :], q_ref[0, :, h * LANES:(h + 1) * LANES])

        def update(h, s):
            if visible is not None:
                s = jnp.where(visible, s, -jnp.inf)
            m = m_ref[h:h + 1, :]
            m_new = jnp.maximum(m, jnp.max(s, axis=0, keepdims=True))
            alpha = jnp.exp2(m - m_new)
            p = jnp.exp2(s - m_new)
            m_ref[h:h + 1, :] = m_new
            v1 = jnp.concatenate([v_ref[0, h, j], ones], axis=0)
            acc_ref[h] = alpha * acc_ref[h] + _dot(v1, p.astype(BF16))

        _pipelined_heads(score, update, None)

    for d in range(ratio):
        step(i * ratio + d, ((key + d * tk) // CHUNK) <= (qry // CHUNK))

    def body(n, _):
        step(n, None)
        return 0

    lax.fori_loop(0, i * ratio, body, 0)
    _store_heads([acc_ref[h, :HEAD_DIM] / acc_ref[h, HEAD_DIM:HEAD_DIM + 1] for h in range(N_HEADS)],
                 g_ref, gate_ref, o_ref)


def _attention(body, name, tq, q, k, v, gate, g, k_spec, acc_rows, stash):
    B, S, _ = gate.shape
    tq = min(tq, S)
    assert S % tq == 0 and tq % v.shape[-1] == 0
    return pl.pallas_call(
        body,
        grid=(B, S // tq),
        in_specs=[pl.BlockSpec((1, tq, q.shape[-1]), lambda b, i: (b, i, 0)),
                  k_spec,
                  pl.BlockSpec((1,) + v.shape[1:], lambda b, i: (b, 0, 0, 0, 0)),
                  pl.BlockSpec((1, tq, D_GROUP), lambda b, i: (b, i, 0)),
                  pl.BlockSpec((1, D_GROUP), lambda b, i: (0, 0))],
        out_specs=pl.BlockSpec((1, tq, D_GROUP), lambda b, i: (b, i, 0)),
        out_shape=jax.ShapeDtypeStruct((B, S, D_GROUP), BF16),
        scratch_shapes=[pltpu.VMEM((N_HEADS, acc_rows, tq), F32), pltpu.VMEM((N_HEADS, tq), F32)]
        + [pltpu.VMEM((LOOKAHEAD, v.shape[-1], tq), F32)] * stash,
        compiler_params=pltpu.CompilerParams(dimension_semantics=("arbitrary", "arbitrary"),
                                             vmem_limit_bytes=VMEM_LIMIT),
        name=name,
    )(q, k, v, gate, g)


def _out_kernel(sb_ref, mla_ref, x_ref, p_ref, woa_ref, wob_ref, gpost_ref, wple_ref, gple_ref, wpg_ref, bpg_ref,
                o_ref):
    y = _dot(sb_ref[...], woa_ref[...]) + _dot(mla_ref[...], wob_ref[...])
    x1 = x_ref[...] + _rms(y, gpost_ref[...])
    ple = _rms(_dot(p_ref[...].astype(BF16), wple_ref[...]), gple_ref[...])
    gate = _sigmoid(_dot(x1.astype(BF16), wpg_ref[...]) + bpg_ref[...])
    o_ref[...] = x1 + ple * gate


def _output(sb_y, mla_y, x, p, woa, wob, gpost, wple, gple, wpg, bpg):
    B, S, D = x.shape
    tm = min(PROJ_ROWS, S)
    full = lambda a: pl.BlockSpec(a.shape, lambda b, t: (0,) * a.ndim)
    row = lambda w: pl.BlockSpec((None, tm, w), lambda b, t: (b, t, 0))
    weights = (woa, wob, gpost, wple, gple, wpg, bpg)
    return pl.pallas_call(
        _out_kernel,
        grid=(B, S // tm),
        in_specs=[row(D_GROUP), row(D_GROUP), row(D), row(p.shape[-1])] + [full(w) for w in weights],
        out_specs=row(D),
        out_shape=jax.ShapeDtypeStruct((B, S, D), F32),
        compiler_params=pltpu.CompilerParams(dimension_semantics=("arbitrary", "arbitrary"),
                                             vmem_limit_bytes=VMEM_LIMIT),
        name="proj_out",
    )(sb_y, mla_y, x, p, *weights)


def _rotate_half_cols(w):
    half = w.shape[-1] // 2
    return jnp.concatenate([-w[..., half:], w[..., :half]], axis=-1)


def _head_lanes(nope, rope):
    K = nope.shape[0]
    pad = jnp.zeros((K, N_HEADS, LANES - QK_NOPE_DIM - QK_ROPE_DIM), nope.dtype)
    return jnp.concatenate([nope, rope, pad], axis=-1).reshape(K, N_HEADS * LANES)


def _layer(x, p, pos, norm_pre_g, w_in, q_norm_g, w_uq, kv_norm_g, w_ukv, sb_out_norm_g, mla_out_norm_g, w_out,
           norm_post_g, w_ple, ple_norm_g, w_ple_gate, b_ple_gate):
    S = x.shape[1]
    row = lambda a: a.reshape(1, -1).astype(F32)
    c = np.cumsum([0, D_GROUP, D_GROUP, D_GROUP, D_GROUP, Q_LORA_RANK, KV_LORA_RANK, QK_ROPE_DIM, D_GROUP])
    wq, wk, wv, wg, wcq, wckv, wkr, wmg = (w_in[:, c[n]:c[n + 1]] for n in range(8))

    def rope_lanes(w):
        return jnp.pad(w, ((0, 0), (QK_NOPE_DIM, LANES - QK_NOPE_DIM - QK_ROPE_DIM)))

    wkr2 = jnp.concatenate([rope_lanes(wkr), rope_lanes(_rotate_half_cols(wkr))], axis=-1)
    uq = w_uq.reshape(Q_LORA_RANK, N_HEADS, QK_NOPE_DIM + QK_ROPE_DIM)
    uq_nope, uq_rope = uq[..., :QK_NOPE_DIM], uq[..., QK_NOPE_DIM:]
    wuq2 = jnp.concatenate([_head_lanes(uq_nope, uq_rope),
                            _head_lanes(jnp.zeros_like(uq_nope), _rotate_half_cols(uq_rope))], axis=-1)
    ukv = w_ukv.reshape(KV_LORA_RANK, N_HEADS, QK_NOPE_DIM + HEAD_DIM)
    uk, uv = ukv[..., :QK_NOPE_DIM], ukv[..., QK_NOPE_DIM:]
    wukv2 = jnp.concatenate([_head_lanes(uk, jnp.zeros((KV_LORA_RANK, N_HEADS, QK_ROPE_DIM), uk.dtype)),
                             uv.reshape(KV_LORA_RANK, D_GROUP)], axis=-1)
    half = QK_ROPE_DIM // 2
    freq = (ROPE_THETA ** (-jnp.arange(half, dtype=F32) / half)).reshape(half, 1)
    bf = lambda a: a.astype(BF16)

    sbq, sbk, sbv, sbg, mq, mk, mv, mg = _projections(
        x, pos.astype(F32)[:, None, :], freq, row(norm_pre_g), bf(wq), bf(wk), bf(wv), bf(wg), bf(wcq), bf(wckv),
        bf(wkr2), bf(wmg), row(q_norm_g), bf(wuq2), row(kv_norm_g), bf(wukv2))
    sb_k_spec = pl.BlockSpec((1, S, D_GROUP), lambda b, i: (b, 0, 0))
    mla_k_spec = pl.BlockSpec((1, N_HEADS, S, LANES), lambda b, i: (b, 0, 0, 0))
    sb_y = _attention(_sb_kernel, "sb_attn", SB_TQ, sbq, sbk, sbv, sbg, row(sb_out_norm_g), sb_k_spec, HEAD_DIM, 1)
    mla_y = _attention(_mla_kernel, "mla_attn", MLA_TQ, mq, mk, mv, mg, row(mla_out_norm_g), mla_k_spec,
                       HEAD_DIM + SUM_ROWS, 0)
    return _output(sb_y, mla_y, x, p, bf(w_out[:D_GROUP]), bf(w_out[D_GROUP:]), row(norm_post_g), bf(w_ple),
                   row(ple_norm_g), bf(w_ple_gate), row(b_ple_gate))


def kernel(x, p, positions, norm_pre_g, w_in, q_norm_g, w_uq, kv_norm_g, w_ukv, sb_out_norm_g, mla_out_norm_g, w_out,
           norm_post_g, w_ple, ple_norm_g, w_ple_gate, b_ple_gate):
    for i in range(p.shape[0]):
        x = _layer(x, p[i], positions, norm_pre_g[i], w_in[i], q_norm_g[i], w_uq[i], kv_norm_g[i], w_ukv[i],
                   sb_out_norm_g[i], mla_out_norm_g[i], w_out[i], norm_post_g[i], w_ple[i], ple_norm_g[i],
                   w_ple_gate[i], b_ple_gate[i])
    return x
```

```python
import jax
import jax.numpy as jnp
import numpy as np
from jax import lax
from jax.experimental import pallas as pl
from jax.experimental.pallas import tpu as pltpu

HEAD_DIM = 64
N_HEADS = 8
N_PAIRS = N_HEADS // 2
LANES = 128
QK_NOPE_DIM = 64
QK_ROPE_DIM = 32
Q_LORA_RANK = 256
KV_LORA_RANK = 128
CHUNK = 64
ROPE_THETA = 10000.0
EPS = 1e-6
D_GROUP = N_HEADS * HEAD_DIM
LOG2E = 1.4426950408889634

PROJ_ROWS = 512
SB_TK = 256
SB_TQ = 256
MLA_TK = 512
MLA_TQ = 1024
SUM_ROWS = 16
LOOKAHEAD = 3
SKIP_LOG2 = 152.0
EXP2_CLAMP = 126.0
VMEM_LIMIT = 56 * 1024 * 1024

F32 = jnp.float32
BF16 = jnp.bfloat16


def _rms(x, g):
    return x * lax.rsqrt(jnp.mean(x * x, axis=-1, keepdims=True) + EPS) * g


def _dot(a, b):
    return jnp.dot(a, b, preferred_element_type=F32)


def _dot_nt(a, b):
    return lax.dot_general(a, b, (((1,), (1,)), ((), ())), preferred_element_type=F32)


def _sigmoid(x):
    return 1.0 / (1.0 + jnp.exp(-x))


def _proj_kernel(x_ref, pos_ref, freq_ref, gpre_ref, wq_ref, wk_ref, wv_ref, wg_ref, wcq_ref, wckv_ref,
                 wkr_ref, wmg_ref, gq_ref, wuq_ref, gkv_ref, wukv_ref,
                 sbq_ref, sbk_ref, sbv_ref, sbg_ref, mq_ref, mk_ref, mv_ref, mg_ref):
    h = _rms(x_ref[...], gpre_ref[...]).astype(BF16)

    def store_transposed(a, out_ref):
        tk = out_ref.shape[-1]
        for p in range(N_PAIRS):
            at = a[:, p * LANES:(p + 1) * LANES].T.astype(BF16)
            for c in range(out_ref.shape[2]):
                out_ref[0, 2 * p, c] = at[:HEAD_DIM, c * tk:(c + 1) * tk]
                out_ref[0, 2 * p + 1, c] = at[HEAD_DIM:, c * tk:(c + 1) * tk]

    sbq_ref[0] = (_dot(h, wq_ref[...]) * (HEAD_DIM ** -0.5 * LOG2E)).astype(BF16)
    sbk_ref[0] = _dot(h, wk_ref[...]).astype(BF16)
    store_transposed(_dot(h, wv_ref[...]), sbv_ref)
    sbg_ref[0] = _dot(h, wg_ref[...])
    mg_ref[0] = _dot(h, wmg_ref[...])

    ang_t = freq_ref[...] * pos_ref[0]
    n_freq = ang_t.shape[0]
    frow = lax.broadcasted_iota(jnp.int32, (n_freq, LANES), 0)
    rope_lane = lax.broadcasted_iota(jnp.int32, (n_freq, LANES), 1) - QK_NOPE_DIM
    spread = ((rope_lane == frow) | (rope_lane - n_freq == frow)).astype(BF16)

    def on_lanes(t):
        hi = t.astype(BF16)
        lo = (t - hi.astype(F32)).astype(BF16)
        tn = lambda a: lax.dot_general(a, spread, (((0,), (0,)), ((), ())), preferred_element_type=F32)
        return tn(hi) + tn(lo)

    nope_lanes = (lax.broadcasted_iota(jnp.int32, (1, LANES), 1) < QK_NOPE_DIM).astype(F32)
    cos, sin = on_lanes(jnp.cos(ang_t)) + nope_lanes, on_lanes(jnp.sin(ang_t))

    cq = _rms(_dot(h, wcq_ref[...]), gq_ref[...]).astype(BF16)
    q2 = _dot(cq, wuq_ref[...])
    scale = (QK_NOPE_DIM + QK_ROPE_DIM) ** -0.5 * LOG2E
    cos_q, sin_q = cos * scale, sin * scale
    half = N_HEADS * LANES
    for hd in range(N_HEADS):
        qa = q2[:, hd * LANES:(hd + 1) * LANES]
        qb = q2[:, half + hd * LANES:half + (hd + 1) * LANES]
        mq_ref[0, :, hd * LANES:(hd + 1) * LANES] = (qa * cos_q + qb * sin_q).astype(BF16)

    ckv = _rms(_dot(h, wckv_ref[...]), gkv_ref[...]).astype(BF16)
    kv = _dot(ckv, wukv_ref[...])
    kr2 = _dot(h, wkr_ref[...])
    kr = kr2[:, :LANES] * cos + kr2[:, LANES:] * sin
    for hd in range(N_HEADS):
        mk_ref[0, hd] = (kv[:, hd * LANES:(hd + 1) * LANES] + kr).astype(BF16)
    store_transposed(kv[:, half:], mv_ref)


def _projections(x, pos, freq, gpre, wq, wk, wv, wg, wcq, wckv, wkr, wmg, gq, wuq, gkv, wukv):
    B, S, D = x.shape
    tm = min(PROJ_ROWS, S)
    assert S % tm == 0
    nt = S // tm
    full = lambda a: pl.BlockSpec(a.shape, lambda b, t: (0,) * a.ndim)
    row = lambda w: pl.BlockSpec((1, tm, w), lambda b, t: (b, t, 0))
    head = pl.BlockSpec((1, N_HEADS, tm, LANES), lambda b, t: (b, 0, t, 0))

    def vt(tile):
        tk = min(tile, S)
        assert tm % tk == 0
        return (pl.BlockSpec((1, N_HEADS, tm // tk, HEAD_DIM, tk), lambda b, t: (b, 0, t, 0, 0)),
                jax.ShapeDtypeStruct((B, N_HEADS, S // tk, HEAD_DIM, tk), BF16))

    (sb_vt, sb_vt_shape), (mla_vt, mla_vt_shape) = vt(SB_TK), vt(MLA_TK)
    weights = (freq, gpre, wq, wk, wv, wg, wcq, wckv, wkr, wmg, gq, wuq, gkv, wukv)
    return pl.pallas_call(
        _proj_kernel,
        grid=(B, nt),
        in_specs=[pl.BlockSpec((None, tm, D), lambda b, t: (b, t, 0)),
                  pl.BlockSpec((1, 1, tm), lambda b, t: (b, 0, t))] + [full(w) for w in weights],
        out_specs=[row(D_GROUP), row(D_GROUP), sb_vt, row(D_GROUP), row(N_HEADS * LANES), head, mla_vt, row(D_GROUP)],
        out_shape=[jax.ShapeDtypeStruct((B, S, D_GROUP), BF16), jax.ShapeDtypeStruct((B, S, D_GROUP), BF16),
                   sb_vt_shape,
                   jax.ShapeDtypeStruct((B, S, D_GROUP), F32),
                   jax.ShapeDtypeStruct((B, S, N_HEADS * LANES), BF16),
                   jax.ShapeDtypeStruct((B, N_HEADS, S, LANES), BF16), mla_vt_shape,
                   jax.ShapeDtypeStruct((B, S, D_GROUP), F32)],
        compiler_params=pltpu.CompilerParams(dimension_semantics=("arbitrary", "arbitrary"),
                                             vmem_limit_bytes=VMEM_LIMIT),
        name="proj_in",
    )(x, pos, *weights)


def _store_heads(o_t, g_ref, gate_ref, o_ref):
    for p in range(N_PAIRS):
        normed = [o * lax.rsqrt(jnp.mean(o * o, axis=0, keepdims=True) + EPS) for o in o_t[2 * p:2 * p + 2]]
        o_pair = jnp.concatenate(normed, axis=0).T
        cols = slice(p * LANES, (p + 1) * LANES)
        gate = gate_ref[0, :, cols]
        o_ref[0, :, cols] = (o_pair * g_ref[:, cols] * (gate * _sigmoid(gate))).astype(o_ref.dtype)


def _pipelined_heads(score, first, second, n_units=N_HEADS, stash_ref=None, stashed=False, score_next=None):
    s = [stash_ref[u] if stashed else score(u) for u in range(LOOKAHEAD)]
    mid = []
    for u in range(n_units):
        mid.append(first(u, s[u]))
        if u + LOOKAHEAD < n_units:
            s.append(score(u + LOOKAHEAD))
        elif stash_ref is not None:
            stash_ref[u + LOOKAHEAD - n_units] = score_next(u + LOOKAHEAD - n_units)
        if second is not None and u > 0:
            second(u - 1, s[u - 1], mid[u - 1])
    if second is not None:
        second(n_units - 1, s[-1], mid[-1])


def _sb_kernel(q_ref, k_ref, v_ref, gate_ref, g_ref, o_ref, acc_ref, carry_ref, stash_ref):
    tq = q_ref.shape[1]
    tk = v_ref.shape[-1]
    assert tq == tk
    i = pl.program_id(1)
    low_half = lax.broadcasted_iota(jnp.int32, (tq, LANES), 1) < HEAD_DIM
    qs = []
    for p in range(N_PAIRS):
        qp = q_ref[0, :, p * LANES:(p + 1) * LANES]
        qs += [jnp.where(low_half, qp, 0), jnp.where(low_half, 0, qp)]
    upper = (lax.broadcasted_iota(jnp.int32, (tk, tk), 1) >= lax.broadcasted_iota(jnp.int32, (tk, tk), 0)).astype(BF16)

    acc_ref[...] = jnp.zeros_like(acc_ref)
    carry_ref[...] = jnp.zeros_like(carry_ref)

    def raw_score(h, rows, q0, qn):
        pair = slice((h // 2) * LANES, (h // 2 + 1) * LANES)
        return _dot_nt(k_ref[0, rows, pair], qs[h][q0:q0 + qn])

    def step(blocks, j_next, stashed):
        def split(u):
            return blocks[u // N_HEADS], u % N_HEADS

        def score(u):
            (j, k0, kn, q0, qn, masked), h = split(u)
            z = raw_score(h, pl.ds(pl.multiple_of(j * tk + k0, kn), kn), q0, qn)
            if masked:
                past = (lax.broadcasted_iota(jnp.int32, (kn, qn), 0) + k0
                        < lax.broadcasted_iota(jnp.int32, (kn, qn), 1) + q0)
                z = jnp.where(past, z, -jnp.inf)
            return z

        def suffix(u, z):
            kn = z.shape[0]
            sp = jnp.maximum(jnp.log2(1.0 + jnp.exp2(jnp.minimum(z, EXP2_CLAMP))), z)
            return _dot(upper[:kn, :kn], sp.astype(BF16))

        def weigh(u, z, c):
            (j, k0, kn, q0, qn, _), h = split(u)
            carry = carry_ref[h:h + 1, q0:q0 + qn]
            w = jnp.exp2(z - c - carry)
            acc_ref[h, :, q0:q0 + qn] += _dot(v_ref[0, h, j, :, k0:k0 + kn], w.astype(BF16))
            carry_ref[h:h + 1, q0:q0 + qn] = carry + c[0:1, :]

        next_rows = pl.ds(pl.multiple_of(j_next * tk, tk), tk)
        _pipelined_heads(score, suffix, weigh, n_units=len(blocks) * N_HEADS, stash_ref=stash_ref, stashed=stashed,
                         score_next=lambda h: raw_score(h, next_rows, 0, tq))

    step([(i, 0, tk, 0, tq, True)], jnp.maximum(i - 1, 0), False)

    def body(state):
        n, _ = state
        step([(i - 1 - n, 0, tk, 0, tq, False)], jnp.maximum(i - 2 - n, 0), True)
        return n + 1, jnp.min(carry_ref[...])

    lax.while_loop(lambda st: (st[0] < i) & (st[1] < SKIP_LOG2), body, (jnp.int32(0), jnp.float32(0.0)))
    _store_heads([acc_ref[h] for h in range(N_HEADS)], g_ref, gate_ref, o_ref)


def _mla_kernel(q_ref, k_ref, v_ref, gate_ref, g_ref, o_ref, acc_ref, m_ref):
    tq = q_ref.shape[1]
    tk = v_ref.shape[-1]
    ratio, half = tq // tk, tq // 2
    assert half <= tk and tk % half == 0
    i = pl.program_id(1)

    acc_ref[...] = jnp.zeros_like(acc_ref)
    m_ref[...] = jnp.full(m_ref.shape, -jnp.inf, F32)

    def step(blocks):
        def split(u):
            return blocks[u // N_HEADS], u % N_HEADS

        def score(u):
            (j, k0, kn, q0, qn, _), h = split(u)
            rows = pl.ds(pl.multiple_of(j * tk + k0, kn), kn)
            return _dot_nt(k_ref[0, h, rows, :], q_ref[0, q0:q0 + qn, h * LANES:(h + 1) * LANES])

        def update(u, s):
            (j, k0, kn, q0, qn, key_off), h = split(u)
            if key_off is not None:
                key = lax.broadcasted_iota(jnp.int32, (kn, qn), 0) + key_off
                qry = lax.broadcasted_iota(jnp.int32, (kn, qn), 1) + q0
                s = jnp.where((key // CHUNK) <= (qry // CHUNK), s, -jnp.inf)
            m = m_ref[h:h + 1, q0:q0 + qn]
            m_new = jnp.maximum(m, jnp.max(s, axis=0, keepdims=True))
            alpha = jnp.exp2(m - m_new)
            p = jnp.exp2(s - m_new)
            m_ref[h:h + 1, q0:q0 + qn] = m_new
            v1 = jnp.concatenate([v_ref[0, h, j, :, k0:k0 + kn], jnp.ones((SUM_ROWS, kn), BF16)], axis=0)
            acc_ref[h, :, q0:q0 + qn] = alpha * acc_ref[h, :, q0:q0 + qn] + _dot(v1, p.astype(BF16))

        _pipelined_heads(score, update, None, n_units=len(blocks) * N_HEADS)

    step([(i * ratio, 0, half, 0, tq, 0), (i * ratio + half // tk, half % tk, half, half, half, half)])

    def body(n, _):
        step([(n, 0, tk, 0, tq, None)])
        return 0

    lax.fori_loop(0, i * ratio, body, 0)
    _store_heads([acc_ref[h, :HEAD_DIM] / acc_ref[h, HEAD_DIM:HEAD_DIM + 1] for h in range(N_HEADS)],
                 g_ref, gate_ref, o_ref)


def _attention(body, name, tq, q, k, v, gate, g, k_spec, acc_rows, stash):
    B, S, _ = gate.shape
    tq = min(tq, S)
    assert S % tq == 0 and tq % v.shape[-1] == 0
    return pl.pallas_call(
        body,
        grid=(B, S // tq),
        in_specs=[pl.BlockSpec((1, tq, q.shape[-1]), lambda b, i: (b, i, 0)),
                  k_spec,
                  pl.BlockSpec((1,) + v.shape[1:], lambda b, i: (b, 0, 0, 0, 0)),
                  pl.BlockSpec((1, tq, D_GROUP), lambda b, i: (b, i, 0)),
                  pl.BlockSpec((1, D_GROUP), lambda b, i: (0, 0))],
        out_specs=pl.BlockSpec((1, tq, D_GROUP), lambda b, i: (b, i, 0)),
        out_shape=jax.ShapeDtypeStruct((B, S, D_GROUP), BF16),
        scratch_shapes=[pltpu.VMEM((N_HEADS, acc_rows, tq), F32), pltpu.VMEM((N_HEADS, tq), F32)]
        + [pltpu.VMEM((LOOKAHEAD, v.shape[-1], tq), F32)] * stash,
        compiler_params=pltpu.CompilerParams(dimension_semantics=("arbitrary", "arbitrary"),
                                             vmem_limit_bytes=VMEM_LIMIT),
        name=name,
    )(q, k, v, gate, g)


def _out_kernel(sb_ref, mla_ref, x_ref, p_ref, woa_ref, wob_ref, gpost_ref, wple_ref, gple_ref, wpg_ref, bpg_ref,
                o_ref):
    y = _dot(sb_ref[...], woa_ref[...]) + _dot(mla_ref[...], wob_ref[...])
    x1 = x_ref[...] + _rms(y, gpost_ref[...])
    ple = _rms(_dot(p_ref[...].astype(BF16), wple_ref[...]), gple_ref[...])
    gate = _sigmoid(_dot(x1.astype(BF16), wpg_ref[...]) + bpg_ref[...])
    o_ref[...] = x1 + ple * gate


def _output(sb_y, mla_y, x, p, woa, wob, gpost, wple, gple, wpg, bpg):
    B, S, D = x.shape
    tm = min(PROJ_ROWS, S)
    full = lambda a: pl.BlockSpec(a.shape, lambda b, t: (0,) * a.ndim)
    row = lambda w: pl.BlockSpec((None, tm, w), lambda b, t: (b, t, 0))
    weights = (woa, wob, gpost, wple, gple, wpg, bpg)
    return pl.pallas_call(
        _out_kernel,
        grid=(B, S // tm),
        in_specs=[row(D_GROUP), row(D_GROUP), row(D), row(p.shape[-1])] + [full(w) for w in weights],
        out_specs=row(D),
        out_shape=jax.ShapeDtypeStruct((B, S, D), F32),
        compiler_params=pltpu.CompilerParams(dimension_semantics=("arbitrary", "arbitrary"),
                                             vmem_limit_bytes=VMEM_LIMIT),
        name="proj_out",
    )(sb_y, mla_y, x, p, *weights)


def _rotate_half_cols(w):
    half = w.shape[-1] // 2
    return jnp.concatenate([-w[..., half:], w[..., :half]], axis=-1)


def _head_lanes(nope, rope):
    K = nope.shape[0]
    pad = jnp.zeros((K, N_HEADS, LANES - QK_NOPE_DIM - QK_ROPE_DIM), nope.dtype)
    return jnp.concatenate([nope, rope, pad], axis=-1).reshape(K, N_HEADS * LANES)


def _layer(x, p, pos, norm_pre_g, w_in, q_norm_g, w_uq, kv_norm_g, w_ukv, sb_out_norm_g, mla_out_norm_g, w_out,
           norm_post_g, w_ple, ple_norm_g, w_ple_gate, b_ple_gate):
    S = x.shape[1]
    row = lambda a: a.reshape(1, -1).astype(F32)
    c = np.cumsum([0, D_GROUP, D_GROUP, D_GROUP, D_GROUP, Q_LORA_RANK, KV_LORA_RANK, QK_ROPE_DIM, D_GROUP])
    wq, wk, wv, wg, wcq, wckv, wkr, wmg = (w_in[:, c[n]:c[n + 1]] for n in range(8))

    def rope_lanes(w):
        return jnp.pad(w, ((0, 0), (QK_NOPE_DIM, LANES - QK_NOPE_DIM - QK_ROPE_DIM)))

    wkr2 = jnp.concatenate([rope_lanes(wkr), rope_lanes(_rotate_half_cols(wkr))], axis=-1)
    uq = w_uq.reshape(Q_LORA_RANK, N_HEADS, QK_NOPE_DIM + QK_ROPE_DIM)
    uq_nope, uq_rope = uq[..., :QK_NOPE_DIM], uq[..., QK_NOPE_DIM:]
    wuq2 = jnp.concatenate([_head_lanes(uq_nope, uq_rope),
                            _head_lanes(jnp.zeros_like(uq_nope), _rotate_half_cols(uq_rope))], axis=-1)
    ukv = w_ukv.reshape(KV_LORA_RANK, N_HEADS, QK_NOPE_DIM + HEAD_DIM)
    uk, uv = ukv[..., :QK_NOPE_DIM], ukv[..., QK_NOPE_DIM:]
    wukv2 = jnp.concatenate([_head_lanes(uk, jnp.zeros((KV_LORA_RANK, N_HEADS, QK_ROPE_DIM), uk.dtype)),
                             uv.reshape(KV_LORA_RANK, D_GROUP)], axis=-1)
    half = QK_ROPE_DIM // 2
    freq = (ROPE_THETA ** (-jnp.arange(half, dtype=F32) / half)).reshape(half, 1)
    bf = lambda a: a.astype(BF16)

    sbq, sbk, sbv, sbg, mq, mk, mv, mg = _projections(
        x, pos.astype(F32)[:, None, :], freq, row(norm_pre_g), bf(wq), bf(wk), bf(wv), bf(wg), bf(wcq), bf(wckv),
        bf(wkr2), bf(wmg), row(q_norm_g), bf(wuq2), row(kv_norm_g), bf(wukv2))
    sb_k_spec = pl.BlockSpec((1, S, D_GROUP), lambda b, i: (b, 0, 0))
    mla_k_spec = pl.BlockSpec((1, N_HEADS, S, LANES), lambda b, i: (b, 0, 0, 0))
    sb_y = _attention(_sb_kernel, "sb_attn", SB_TQ, sbq, sbk, sbv, sbg, row(sb_out_norm_g), sb_k_spec, HEAD_DIM, 1)
    mla_y = _attention(_mla_kernel, "mla_attn", MLA_TQ, mq, mk, mv, mg, row(mla_out_norm_g), mla_k_spec,
                       HEAD_DIM + SUM_ROWS, 0)
    return _output(sb_y, mla_y, x, p, bf(w_out[:D_GROUP]), bf(w_out[D_GROUP:]), row(norm_post_g), bf(w_ple),
                   row(ple_norm_g), bf(w_ple_gate), row(b_ple_gate))


def kernel(x, p, positions, norm_pre_g, w_in, q_norm_g, w_uq, kv_norm_g, w_ukv, sb_out_norm_g, mla_out_norm_g, w_out,
           norm_post_g, w_ple, ple_norm_g, w_ple_gate, b_ple_gate):
    for i in range(p.shape[0]):
        x = _layer(x, p[i], positions, norm_pre_g[i], w_in[i], q_norm_g[i], w_uq[i], kv_norm_g[i], w_ukv[i],
                   sb_out_norm_g[i], mla_out_norm_g[i], w_out[i], norm_post_g[i], w_ple[i], ple_norm_g[i],
                   w_ple_gate[i], b_ple_gate[i])
    return x
```

```python
import jax
import jax.numpy as jnp
import numpy as np
from jax import lax
from jax.experimental import pallas as pl
from jax.experimental.pallas import tpu as pltpu

HEAD_DIM = 64
N_HEADS = 8
N_PAIRS = N_HEADS // 2
LANES = 128
QK_NOPE_DIM = 64
QK_ROPE_DIM = 32
Q_LORA_RANK = 256
KV_LORA_RANK = 128
CHUNK = 64
ROPE_THETA = 10000.0
EPS = 1e-6
D_GROUP = N_HEADS * HEAD_DIM
LOG2E = 1.4426950408889634

PROJ_ROWS = 1024
SB_TK = 256
SB_TQ = 256
MLA_TK = 512
MLA_TQ = 1024
SUM_ROWS = 16
LOOKAHEAD = 3
SKIP_LOG2 = 152.0
EXP2_CLAMP = 126.0
VMEM_LIMIT = 56 * 1024 * 1024

F32 = jnp.float32
BF16 = jnp.bfloat16


def _rms(x, g):
    return x * lax.rsqrt(jnp.mean(x * x, axis=-1, keepdims=True) + EPS) * g


def _dot(a, b):
    return jnp.dot(a, b, preferred_element_type=F32)


def _dot_nt(a, b):
    return lax.dot_general(a, b, (((1,), (1,)), ((), ())), preferred_element_type=F32)


def _sigmoid(x):
    return 1.0 / (1.0 + jnp.exp(-x))


def _proj_kernel(x_ref, pos_ref, freq_ref, gpre_ref, wq_ref, wk_ref, wv_ref, wg_ref, wcq_ref, wckv_ref,
                 wkr_ref, wmg_ref, gq_ref, wuq_ref, gkv_ref, wukv_ref,
                 sbq_ref, sbk_ref, sbv_ref, sbg_ref, mq_ref, mk_ref, mv_ref, mg_ref):
    h = _rms(x_ref[...], gpre_ref[...]).astype(BF16)

    def store_transposed(a, out_ref):
        tk = out_ref.shape[-1]
        for p in range(N_PAIRS):
            at = a[:, p * LANES:(p + 1) * LANES].T.astype(BF16)
            for c in range(out_ref.shape[2]):
                out_ref[0, 2 * p, c] = at[:HEAD_DIM, c * tk:(c + 1) * tk]
                out_ref[0, 2 * p + 1, c] = at[HEAD_DIM:, c * tk:(c + 1) * tk]

    sbq_ref[0] = (_dot(h, wq_ref[...]) * (HEAD_DIM ** -0.5 * LOG2E)).astype(BF16)
    sbk_ref[0] = _dot(h, wk_ref[...]).astype(BF16)
    store_transposed(_dot(h, wv_ref[...]), sbv_ref)
    sbg_ref[0] = _dot(h, wg_ref[...])
    mg_ref[0] = _dot(h, wmg_ref[...])

    ang_t = freq_ref[...] * pos_ref[0]
    n_freq = ang_t.shape[0]
    frow = lax.broadcasted_iota(jnp.int32, (n_freq, LANES), 0)
    rope_lane = lax.broadcasted_iota(jnp.int32, (n_freq, LANES), 1) - QK_NOPE_DIM
    spread = ((rope_lane == frow) | (rope_lane - n_freq == frow)).astype(BF16)

    def on_lanes(t):
        hi = t.astype(BF16)
        lo = (t - hi.astype(F32)).astype(BF16)
        tn = lambda a: lax.dot_general(a, spread, (((0,), (0,)), ((), ())), preferred_element_type=F32)
        return tn(hi) + tn(lo)

    nope_lanes = (lax.broadcasted_iota(jnp.int32, (1, LANES), 1) < QK_NOPE_DIM).astype(F32)
    cos, sin = on_lanes(jnp.cos(ang_t)) + nope_lanes, on_lanes(jnp.sin(ang_t))

    cq = _rms(_dot(h, wcq_ref[...]), gq_ref[...]).astype(BF16)
    q2 = _dot(cq, wuq_ref[...])
    scale = (QK_NOPE_DIM + QK_ROPE_DIM) ** -0.5 * LOG2E
    cos_q, sin_q = cos * scale, sin * scale
    half = N_HEADS * LANES
    for hd in range(N_HEADS):
        qa = q2[:, hd * LANES:(hd + 1) * LANES]
        qb = q2[:, half + hd * LANES:half + (hd + 1) * LANES]
        mq_ref[0, :, hd * LANES:(hd + 1) * LANES] = (qa * cos_q + qb * sin_q).astype(BF16)

    ckv = _rms(_dot(h, wckv_ref[...]), gkv_ref[...]).astype(BF16)
    kv = _dot(ckv, wukv_ref[...])
    kr2 = _dot(h, wkr_ref[...])
    kr = kr2[:, :LANES] * cos + kr2[:, LANES:] * sin
    for hd in range(N_HEADS):
        mk_ref[0, hd] = (kv[:, hd * LANES:(hd + 1) * LANES] + kr).astype(BF16)
    store_transposed(kv[:, half:], mv_ref)


def _projections(x, pos, freq, gpre, wq, wk, wv, wg, wcq, wckv, wkr, wmg, gq, wuq, gkv, wukv):
    B, S, D = x.shape
    tm = min(PROJ_ROWS, S)
    assert S % tm == 0
    nt = S // tm
    full = lambda a: pl.BlockSpec(a.shape, lambda b, t: (0,) * a.ndim)
    row = lambda w: pl.BlockSpec((1, tm, w), lambda b, t: (b, t, 0))
    head = pl.BlockSpec((1, N_HEADS, tm, LANES), lambda b, t: (b, 0, t, 0))

    def vt(tile):
        tk = min(tile, S)
        assert tm % tk == 0
        return (pl.BlockSpec((1, N_HEADS, tm // tk, HEAD_DIM, tk), lambda b, t: (b, 0, t, 0, 0)),
                jax.ShapeDtypeStruct((B, N_HEADS, S // tk, HEAD_DIM, tk), BF16))

    (sb_vt, sb_vt_shape), (mla_vt, mla_vt_shape) = vt(SB_TK), vt(MLA_TK)
    weights = (freq, gpre, wq, wk, wv, wg, wcq, wckv, wkr, wmg, gq, wuq, gkv, wukv)
    return pl.pallas_call(
        _proj_kernel,
        grid=(B, nt),
        in_specs=[pl.BlockSpec((None, tm, D), lambda b, t: (b, t, 0)),
                  pl.BlockSpec((1, 1, tm), lambda b, t: (b, 0, t))] + [full(w) for w in weights],
        out_specs=[row(D_GROUP), row(D_GROUP), sb_vt, row(D_GROUP), row(N_HEADS * LANES), head, mla_vt, row(D_GROUP)],
        out_shape=[jax.ShapeDtypeStruct((B, S, D_GROUP), BF16), jax.ShapeDtypeStruct((B, S, D_GROUP), BF16),
                   sb_vt_shape,
                   jax.ShapeDtypeStruct((B, S, D_GROUP), F32),
                   jax.ShapeDtypeStruct((B, S, N_HEADS * LANES), BF16),
                   jax.ShapeDtypeStruct((B, N_HEADS, S, LANES), BF16), mla_vt_shape,
                   jax.ShapeDtypeStruct((B, S, D_GROUP), F32)],
        compiler_params=pltpu.CompilerParams(dimension_semantics=("arbitrary", "arbitrary"),
                                             vmem_limit_bytes=VMEM_LIMIT),
        name="proj_in",
    )(x, pos, *weights)


def _store_heads(o_t, g_ref, gate_ref, o_ref):
    for p in range(N_PAIRS):
        normed = [o * lax.rsqrt(jnp.mean(o * o, axis=0, keepdims=True) + EPS) for o in o_t[2 * p:2 * p + 2]]
        o_pair = jnp.concatenate(normed, axis=0).T
        cols = slice(p * LANES, (p + 1) * LANES)
        gate = gate_ref[0, :, cols]
        o_ref[0, :, cols] = (o_pair * g_ref[:, cols] * (gate * _sigmoid(gate))).astype(o_ref.dtype)


def _pipelined_heads(score, first, second, n_units=N_HEADS, stash_ref=None, stashed=False, score_next=None):
    s = [stash_ref[u] if stashed else score(u) for u in range(LOOKAHEAD)]
    mid = []
    for u in range(n_units):
        mid.append(first(u, s[u]))
        if u + LOOKAHEAD < n_units:
            s.append(score(u + LOOKAHEAD))
        elif stash_ref is not None:
            stash_ref[u + LOOKAHEAD - n_units] = score_next(u + LOOKAHEAD - n_units)
        if second is not None and u > 0:
            second(u - 1, s[u - 1], mid[u - 1])
    if second is not None:
        second(n_units - 1, s[-1], mid[-1])


def _sb_kernel(q_ref, k_ref, v_ref, gate_ref, g_ref, o_ref, acc_ref, carry_ref, stash_ref):
    tq = q_ref.shape[1]
    tk = v_ref.shape[-1]
    assert tq == tk
    i = pl.program_id(1)
    low_half = lax.broadcasted_iota(jnp.int32, (tq, LANES), 1) < HEAD_DIM
    qs = []
    for p in range(N_PAIRS):
        qp = q_ref[0, :, p * LANES:(p + 1) * LANES]
        qs += [jnp.where(low_half, qp, 0), jnp.where(low_half, 0, qp)]
    upper = (lax.broadcasted_iota(jnp.int32, (tk, tk), 1) >= lax.broadcasted_iota(jnp.int32, (tk, tk), 0)).astype(BF16)

    acc_ref[...] = jnp.zeros_like(acc_ref)
    carry_ref[...] = jnp.zeros_like(carry_ref)

    def raw_score(h, rows, q0, qn):
        pair = slice((h // 2) * LANES, (h // 2 + 1) * LANES)
        return _dot_nt(k_ref[0, rows, pair], qs[h][q0:q0 + qn])

    def step(blocks, j_next, stashed):
        def split(u):
            return blocks[u // N_HEADS], u % N_HEADS

        def score(u):
            (j, k0, kn, q0, qn, masked), h = split(u)
            z = raw_score(h, pl.ds(pl.multiple_of(j * tk + k0, kn), kn), q0, qn)
            if masked:
                past = (lax.broadcasted_iota(jnp.int32, (kn, qn), 0) + k0
                        < lax.broadcasted_iota(jnp.int32, (kn, qn), 1) + q0)
                z = jnp.where(past, z, -jnp.inf)
            return z

        def suffix(u, z):
            kn = z.shape[0]
            sp = jnp.maximum(jnp.log2(1.0 + jnp.exp2(jnp.minimum(z, EXP2_CLAMP))), z)
            return _dot(upper[:kn, :kn], sp.astype(BF16))

        def weigh(u, z, c):
            (j, k0, kn, q0, qn, _), h = split(u)
            carry = carry_ref[h:h + 1, q0:q0 + qn]
            w = jnp.exp2(z - c - carry)
            acc_ref[h, :, q0:q0 + qn] += _dot(v_ref[0, h, j, :, k0:k0 + kn], w.astype(BF16))
            carry_ref[h:h + 1, q0:q0 + qn] = carry + c[0:1, :]

        next_rows = pl.ds(pl.multiple_of(j_next * tk, tk), tk)
        _pipelined_heads(score, suffix, weigh, n_units=len(blocks) * N_HEADS, stash_ref=stash_ref, stashed=stashed,
                         score_next=lambda h: raw_score(h, next_rows, 0, tq))

    step([(i, 0, tk, 0, tq, True)], jnp.maximum(i - 1, 0), False)

    def body(state):
        n, _ = state
        step([(i - 1 - n, 0, tk, 0, tq, False)], jnp.maximum(i - 2 - n, 0), True)
        return n + 1, jnp.min(carry_ref[...])

    lax.while_loop(lambda st: (st[0] < i) & (st[1] < SKIP_LOG2), body, (jnp.int32(0), jnp.float32(0.0)))
    _store_heads([acc_ref[h] for h in range(N_HEADS)], g_ref, gate_ref, o_ref)


def _mla_kernel(q_ref, k_ref, v_ref, gate_ref, g_ref, o_ref, acc_ref, m_ref):
    tq = q_ref.shape[1]
    tk = v_ref.shape[-1]
    ratio, half = tq // tk, tq // 2
    assert half <= tk and tk % half == 0
    i = pl.program_id(1)

    acc_ref[...] = jnp.zeros_like(acc_ref)
    m_ref[...] = jnp.full(m_ref.shape, -jnp.inf, F32)

    def step(blocks):
        def split(u):
            return blocks[u // N_HEADS], u % N_HEADS

        def score(u):
            (j, k0, kn, q0, qn, _), h = split(u)
            rows = pl.ds(pl.multiple_of(j * tk + k0, kn), kn)
            return _dot_nt(k_ref[0, h, rows, :], q_ref[0, q0:q0 + qn, h * LANES:(h + 1) * LANES])

        def update(u, s):
            (j, k0, kn, q0, qn, key_off), h = split(u)
            if key_off is not None:
                key = lax.broadcasted_iota(jnp.int32, (kn, qn), 0) + key_off
                qry = lax.broadcasted_iota(jnp.int32, (kn, qn), 1) + q0
                s = jnp.where((key // CHUNK) <= (qry // CHUNK), s, -jnp.inf)
            m = m_ref[h:h + 1, q0:q0 + qn]
            m_new = jnp.maximum(m, jnp.max(s, axis=0, keepdims=True))
            alpha = jnp.exp2(m - m_new)
            p = jnp.exp2(s - m_new)
            m_ref[h:h + 1, q0:q0 + qn] = m_new
            v1 = jnp.concatenate([v_ref[0, h, j, :, k0:k0 + kn], jnp.ones((SUM_ROWS, kn), BF16)], axis=0)
            acc_ref[h, :, q0:q0 + qn] = alpha * acc_ref[h, :, q0:q0 + qn] + _dot(v1, p.astype(BF16))

        _pipelined_heads(score, update, None, n_units=len(blocks) * N_HEADS)

    step([(i * ratio, 0, half, 0, tq, 0), (i * ratio + half // tk, half % tk, half, half, half, half)])

    def body(n, _):
        step([(n, 0, tk, 0, tq, None)])
        return 0

    lax.fori_loop(0, i * ratio, body, 0)
    _store_heads([acc_ref[h, :HEAD_DIM] / acc_ref[h, HEAD_DIM:HEAD_DIM + 1] for h in range(N_HEADS)],
                 g_ref, gate_ref, o_ref)


def _attention(body, name, tq, q, k, v, gate, g, k_spec, acc_rows, stash):
    B, S, _ = gate.shape
    tq = min(tq, S)
    assert S % tq == 0 and tq % v.shape[-1] == 0
    return pl.pallas_call(
        body,
        grid=(B, S // tq),
        in_specs=[pl.BlockSpec((1, tq, q.shape[-1]), lambda b, i: (b, i, 0)),
                  k_spec,
                  pl.BlockSpec((1,) + v.shape[1:], lambda b, i: (b, 0, 0, 0, 0)),
                  pl.BlockSpec((1, tq, D_GROUP), lambda b, i: (b, i, 0)),
                  pl.BlockSpec((1, D_GROUP), lambda b, i: (0, 0))],
        out_specs=pl.BlockSpec((1, tq, D_GROUP), lambda b, i: (b, i, 0)),
        out_shape=jax.ShapeDtypeStruct((B, S, D_GROUP), BF16),
        scratch_shapes=[pltpu.VMEM((N_HEADS, acc_rows, tq), F32), pltpu.VMEM((N_HEADS, tq), F32)]
        + [pltpu.VMEM((LOOKAHEAD, v.shape[-1], tq), F32)] * stash,
        compiler_params=pltpu.CompilerParams(dimension_semantics=("arbitrary", "arbitrary"),
                                             vmem_limit_bytes=VMEM_LIMIT),
        name=name,
    )(q, k, v, gate, g)


def _out_kernel(sb_ref, mla_ref, x_ref, p_ref, woa_ref, wob_ref, gpost_ref, wple_ref, gple_ref, wpg_ref, bpg_ref,
                o_ref):
    y = _dot(sb_ref[...], woa_ref[...]) + _dot(mla_ref[...], wob_ref[...])
    x1 = x_ref[...] + _rms(y, gpost_ref[...])
    ple = _rms(_dot(p_ref[...].astype(BF16), wple_ref[...]), gple_ref[...])
    gate = _sigmoid(_dot(x1.astype(BF16), wpg_ref[...]) + bpg_ref[...])
    o_ref[...] = x1 + ple * gate


def _output(sb_y, mla_y, x, p, woa, wob, gpost, wple, gple, wpg, bpg):
    B, S, D = x.shape
    tm = min(PROJ_ROWS, S)
    full = lambda a: pl.BlockSpec(a.shape, lambda b, t: (0,) * a.ndim)
    row = lambda w: pl.BlockSpec((None, tm, w), lambda b, t: (b, t, 0))
    weights = (woa, wob, gpost, wple, gple, wpg, bpg)
    return pl.pallas_call(
        _out_kernel,
        grid=(B, S // tm),
        in_specs=[row(D_GROUP), row(D_GROUP), row(D), row(p.shape[-1])] + [full(w) for w in weights],
        out_specs=row(D),
        out_shape=jax.ShapeDtypeStruct((B, S, D), F32),
        compiler_params=pltpu.CompilerParams(dimension_semantics=("arbitrary", "arbitrary"),
                                             vmem_limit_bytes=VMEM_LIMIT),
        name="proj_out",
    )(sb_y, mla_y, x, p, *weights)


def _rotate_half_cols(w):
    half = w.shape[-1] // 2
    return jnp.concatenate([-w[..., half:], w[..., :half]], axis=-1)


def _head_lanes(nope, rope):
    K = nope.shape[0]
    pad = jnp.zeros((K, N_HEADS, LANES - QK_NOPE_DIM - QK_ROPE_DIM), nope.dtype)
    return jnp.concatenate([nope, rope, pad], axis=-1).reshape(K, N_HEADS * LANES)


def _layer(x, p, pos, norm_pre_g, w_in, q_norm_g, w_uq, kv_norm_g, w_ukv, sb_out_norm_g, mla_out_norm_g, w_out,
           norm_post_g, w_ple, ple_norm_g, w_ple_gate, b_ple_gate):
    S = x.shape[1]
    row = lambda a: a.reshape(1, -1).astype(F32)
    c = np.cumsum([0, D_GROUP, D_GROUP, D_GROUP, D_GROUP, Q_LORA_RANK, KV_LORA_RANK, QK_ROPE_DIM, D_GROUP])
    wq, wk, wv, wg, wcq, wckv, wkr, wmg = (w_in[:, c[n]:c[n + 1]] for n in range(8))

    def rope_lanes(w):
        return jnp.pad(w, ((0, 0), (QK_NOPE_DIM, LANES - QK_NOPE_DIM - QK_ROPE_DIM)))

    wkr2 = jnp.concatenate([rope_lanes(wkr), rope_lanes(_rotate_half_cols(wkr))], axis=-1)
    uq = w_uq.reshape(Q_LORA_RANK, N_HEADS, QK_NOPE_DIM + QK_ROPE_DIM)
    uq_nope, uq_rope = uq[..., :QK_NOPE_DIM], uq[..., QK_NOPE_DIM:]
    wuq2 = jnp.concatenate([_head_lanes(uq_nope, uq_rope),
                            _head_lanes(jnp.zeros_like(uq_nope), _rotate_half_cols(uq_rope))], axis=-1)
    ukv = w_ukv.reshape(KV_LORA_RANK, N_HEADS, QK_NOPE_DIM + HEAD_DIM)
    uk, uv = ukv[..., :QK_NOPE_DIM], ukv[..., QK_NOPE_DIM:]
    wukv2 = jnp.concatenate([_head_lanes(uk, jnp.zeros((KV_LORA_RANK, N_HEADS, QK_ROPE_DIM), uk.dtype)),
                             uv.reshape(KV_LORA_RANK, D_GROUP)], axis=-1)
    half = QK_ROPE_DIM // 2
    freq = (ROPE_THETA ** (-jnp.arange(half, dtype=F32) / half)).reshape(half, 1)
    bf = lambda a: a.astype(BF16)

    sbq, sbk, sbv, sbg, mq, mk, mv, mg = _projections(
        x, pos.astype(F32)[:, None, :], freq, row(norm_pre_g), bf(wq), bf(wk), bf(wv), bf(wg), bf(wcq), bf(wckv),
        bf(wkr2), bf(wmg), row(q_norm_g), bf(wuq2), row(kv_norm_g), bf(wukv2))
    sb_k_spec = pl.BlockSpec((1, S, D_GROUP), lambda b, i: (b, 0, 0))
    mla_k_spec = pl.BlockSpec((1, N_HEADS, S, LANES), lambda b, i: (b, 0, 0, 0))
    sb_y = _attention(_sb_kernel, "sb_attn", SB_TQ, sbq, sbk, sbv, sbg, row(sb_out_norm_g), sb_k_spec, HEAD_DIM, 1)
    mla_y = _attention(_mla_kernel, "mla_attn", MLA_TQ, mq, mk, mv, mg, row(mla_out_norm_g), mla_k_spec,
                       HEAD_DIM + SUM_ROWS, 0)
    return _output(sb_y, mla_y, x, p, bf(w_out[:D_GROUP]), bf(w_out[D_GROUP:]), row(norm_post_g), bf(w_ple),
                   row(ple_norm_g), bf(w_ple_gate), row(b_ple_gate))


def kernel(x, p, positions, norm_pre_g, w_in, q_norm_g, w_uq, kv_norm_g, w_ukv, sb_out_norm_g, mla_out_norm_g, w_out,
           norm_post_g, w_ple, ple_norm_g, w_ple_gate, b_ple_gate):
    for i in range(p.shape[0]):
        x = _layer(x, p[i], positions, norm_pre_g[i], w_in[i], q_norm_g[i], w_uq[i], kv_norm_g[i], w_ukv[i],
                   sb_out_norm_g[i], mla_out_norm_g[i], w_out[i], norm_post_g[i], w_ple[i], ple_norm_g[i],
                   w_ple_gate[i], b_ple_gate[i])
    return x
```

```python
import jax
import jax.numpy as jnp
import numpy as np
from jax import lax
from jax.experimental import pallas as pl
from jax.experimental.pallas import tpu as pltpu

HEAD_DIM = 64
N_HEADS = 8
N_PAIRS = N_HEADS // 2
LANES = 128
QK_NOPE_DIM = 64
QK_ROPE_DIM = 32
Q_LORA_RANK = 256
KV_LORA_RANK = 128
CHUNK = 64
ROPE_THETA = 10000.0
EPS = 1e-6
D_GROUP = N_HEADS * HEAD_DIM
LOG2E = 1.4426950408889634

PROJ_ROWS = 1024
SB_TK = 256
SB_TQ = 256
MLA_TK = 512
MLA_TQ = 1024
FAST_LOG2 = 60.0
NORM_SLACK = 1.02
SUM_ROWS = 16
LOOKAHEAD = 3
SKIP_LOG2 = 152.0
EXP2_CLAMP = 126.0
VMEM_LIMIT = 56 * 1024 * 1024

F32 = jnp.float32
BF16 = jnp.bfloat16


def _rms(x, g):
    return x * lax.rsqrt(jnp.mean(x * x, axis=-1, keepdims=True) + EPS) * g


def _dot(a, b):
    return jnp.dot(a, b, preferred_element_type=F32)


def _dot_nt(a, b):
    return lax.dot_general(a, b, (((1,), (1,)), ((), ())), preferred_element_type=F32)


def _sigmoid(x):
    return 1.0 / (1.0 + jnp.exp(-x))


def _proj_kernel(x_ref, pos_ref, freq_ref, gpre_ref, wq_ref, wk_ref, wv_ref, wg_ref, wcq_ref, wckv_ref,
                 wkr_ref, wmg_ref, gq_ref, wuq_ref, gkv_ref, wukv_ref,
                 sbq_ref, sbk_ref, sbv_ref, sbg_ref, mq_ref, mk_ref, mv_ref, mg_ref, stats_ref):
    h = _rms(x_ref[...], gpre_ref[...]).astype(BF16)
    n_stat = stats_ref.shape[1]

    def max_sq_norm(a):
        sq = jnp.sum(a * a, axis=-1, keepdims=True)
        rows = a.shape[0] // n_stat
        return [jnp.broadcast_to(jnp.max(sq[c * rows:(c + 1) * rows], axis=0, keepdims=True), (1, LANES))
                for c in range(n_stat)]

    def store_transposed(a, out_ref):
        tk = out_ref.shape[-1]
        for p in range(N_PAIRS):
            at = a[:, p * LANES:(p + 1) * LANES].T.astype(BF16)
            for c in range(out_ref.shape[2]):
                out_ref[0, 2 * p, c] = at[:HEAD_DIM, c * tk:(c + 1) * tk]
                out_ref[0, 2 * p + 1, c] = at[HEAD_DIM:, c * tk:(c + 1) * tk]

    sbq_ref[0] = (_dot(h, wq_ref[...]) * (HEAD_DIM ** -0.5 * LOG2E)).astype(BF16)
    sbk_ref[0] = _dot(h, wk_ref[...]).astype(BF16)
    store_transposed(_dot(h, wv_ref[...]), sbv_ref)
    sbg_ref[0] = _dot(h, wg_ref[...])
    mg_ref[0] = _dot(h, wmg_ref[...])

    ang_t = freq_ref[...] * pos_ref[0]
    n_freq = ang_t.shape[0]
    frow = lax.broadcasted_iota(jnp.int32, (n_freq, LANES), 0)
    rope_lane = lax.broadcasted_iota(jnp.int32, (n_freq, LANES), 1) - QK_NOPE_DIM
    spread = ((rope_lane == frow) | (rope_lane - n_freq == frow)).astype(BF16)

    def on_lanes(t):
        hi = t.astype(BF16)
        lo = (t - hi.astype(F32)).astype(BF16)
        tn = lambda a: lax.dot_general(a, spread, (((0,), (0,)), ((), ())), preferred_element_type=F32)
        return tn(hi) + tn(lo)

    nope_lanes = (lax.broadcasted_iota(jnp.int32, (1, LANES), 1) < QK_NOPE_DIM).astype(F32)
    cos, sin = on_lanes(jnp.cos(ang_t)) + nope_lanes, on_lanes(jnp.sin(ang_t))

    cq = _rms(_dot(h, wcq_ref[...]), gq_ref[...]).astype(BF16)
    q2 = _dot(cq, wuq_ref[...])
    scale = (QK_NOPE_DIM + QK_ROPE_DIM) ** -0.5 * LOG2E
    cos_q, sin_q = cos * scale, sin * scale
    half = N_HEADS * LANES
    q_norms, k_norms = [], []
    for hd in range(N_HEADS):
        qa = q2[:, hd * LANES:(hd + 1) * LANES]
        qb = q2[:, half + hd * LANES:half + (hd + 1) * LANES]
        qh = qa * cos_q + qb * sin_q
        q_norms.append(max_sq_norm(qh))
        mq_ref[0, :, hd * LANES:(hd + 1) * LANES] = qh.astype(BF16)

    ckv = _rms(_dot(h, wckv_ref[...]), gkv_ref[...]).astype(BF16)
    kv = _dot(ckv, wukv_ref[...])
    kr2 = _dot(h, wkr_ref[...])
    kr = kr2[:, :LANES] * cos + kr2[:, LANES:] * sin
    for hd in range(N_HEADS):
        kh = kv[:, hd * LANES:(hd + 1) * LANES] + kr
        k_norms.append(max_sq_norm(kh))
        mk_ref[0, hd] = kh.astype(BF16)
    store_transposed(kv[:, half:], mv_ref)
    for c in range(n_stat):
        stats_ref[0, c] = jnp.concatenate([n[c] for n in k_norms] + [n[c] for n in q_norms], axis=0)


def _projections(x, pos, freq, gpre, wq, wk, wv, wg, wcq, wckv, wkr, wmg, gq, wuq, gkv, wukv):
    B, S, D = x.shape
    tm = min(PROJ_ROWS, S)
    assert S % tm == 0
    nt = S // tm
    full = lambda a: pl.BlockSpec(a.shape, lambda b, t: (0,) * a.ndim)
    row = lambda w: pl.BlockSpec((1, tm, w), lambda b, t: (b, t, 0))
    head = pl.BlockSpec((1, N_HEADS, tm, LANES), lambda b, t: (b, 0, t, 0))

    def vt(tile):
        tk = min(tile, S)
        assert tm % tk == 0
        return (pl.BlockSpec((1, N_HEADS, tm // tk, HEAD_DIM, tk), lambda b, t: (b, 0, t, 0, 0)),
                jax.ShapeDtypeStruct((B, N_HEADS, S // tk, HEAD_DIM, tk), BF16))

    (sb_vt, sb_vt_shape), (mla_vt, mla_vt_shape) = vt(SB_TK), vt(MLA_TK)
    n_stat = tm // min(MLA_TK, S)
    stats = pl.BlockSpec((1, n_stat, 2 * N_HEADS, LANES), lambda b, t: (b, t, 0, 0))
    stats_shape = jax.ShapeDtypeStruct((B, nt * n_stat, 2 * N_HEADS, LANES), F32)
    weights = (freq, gpre, wq, wk, wv, wg, wcq, wckv, wkr, wmg, gq, wuq, gkv, wukv)
    return pl.pallas_call(
        _proj_kernel,
        grid=(B, nt),
        in_specs=[pl.BlockSpec((None, tm, D), lambda b, t: (b, t, 0)),
                  pl.BlockSpec((1, 1, tm), lambda b, t: (b, 0, t))] + [full(w) for w in weights],
        out_specs=[row(D_GROUP), row(D_GROUP), sb_vt, row(D_GROUP), row(N_HEADS * LANES), head, mla_vt, row(D_GROUP),
                   stats],
        out_shape=[jax.ShapeDtypeStruct((B, S, D_GROUP), BF16), jax.ShapeDtypeStruct((B, S, D_GROUP), BF16),
                   sb_vt_shape,
                   jax.ShapeDtypeStruct((B, S, D_GROUP), F32),
                   jax.ShapeDtypeStruct((B, S, N_HEADS * LANES), BF16),
                   jax.ShapeDtypeStruct((B, N_HEADS, S, LANES), BF16), mla_vt_shape,
                   jax.ShapeDtypeStruct((B, S, D_GROUP), F32), stats_shape],
        compiler_params=pltpu.CompilerParams(dimension_semantics=("arbitrary", "arbitrary"),
                                             vmem_limit_bytes=VMEM_LIMIT),
        name="proj_in",
    )(x, pos, *weights)


def _store_heads(o_t, g_ref, gate_ref, o_ref):
    for p in range(N_PAIRS):
        normed = [o * lax.rsqrt(jnp.mean(o * o, axis=0, keepdims=True) + EPS) for o in o_t[2 * p:2 * p + 2]]
        o_pair = jnp.concatenate(normed, axis=0).T
        cols = slice(p * LANES, (p + 1) * LANES)
        gate = gate_ref[0, :, cols]
        o_ref[0, :, cols] = (o_pair * g_ref[:, cols] * (gate * _sigmoid(gate))).astype(o_ref.dtype)


def _pipelined_heads(score, first, second, n_units=N_HEADS, stash_ref=None, stashed=False, score_next=None):
    s = [stash_ref[u] if stashed else score(u) for u in range(LOOKAHEAD)]
    mid = []
    for u in range(n_units):
        mid.append(first(u, s[u]))
        if u + LOOKAHEAD < n_units:
            s.append(score(u + LOOKAHEAD))
        elif stash_ref is not None:
            stash_ref[u + LOOKAHEAD - n_units] = score_next(u + LOOKAHEAD - n_units)
        if second is not None and u > 0:
            second(u - 1, s[u - 1], mid[u - 1])
    if second is not None:
        second(n_units - 1, s[-1], mid[-1])


def _sb_kernel(q_ref, k_ref, v_ref, gate_ref, g_ref, o_ref, acc_ref, carry_ref, stash_ref):
    tq = q_ref.shape[1]
    tk = v_ref.shape[-1]
    assert tq == tk
    i = pl.program_id(1)
    low_half = lax.broadcasted_iota(jnp.int32, (tq, LANES), 1) < HEAD_DIM
    qs = []
    for p in range(N_PAIRS):
        qp = q_ref[0, :, p * LANES:(p + 1) * LANES]
        qs += [jnp.where(low_half, qp, 0), jnp.where(low_half, 0, qp)]
    upper = (lax.broadcasted_iota(jnp.int32, (tk, tk), 1) >= lax.broadcasted_iota(jnp.int32, (tk, tk), 0)).astype(BF16)

    acc_ref[...] = jnp.zeros_like(acc_ref)
    carry_ref[...] = jnp.zeros_like(carry_ref)

    def raw_score(h, rows, q0, qn):
        pair = slice((h // 2) * LANES, (h // 2 + 1) * LANES)
        return _dot_nt(k_ref[0, rows, pair], qs[h][q0:q0 + qn])

    def step(blocks, j_next, stashed):
        def split(u):
            return blocks[u // N_HEADS], u % N_HEADS

        def score(u):
            (j, k0, kn, q0, qn, masked), h = split(u)
            z = raw_score(h, pl.ds(pl.multiple_of(j * tk + k0, kn), kn), q0, qn)
            if masked:
                past = (lax.broadcasted_iota(jnp.int32, (kn, qn), 0) + k0
                        < lax.broadcasted_iota(jnp.int32, (kn, qn), 1) + q0)
                z = jnp.where(past, z, -jnp.inf)
            return z

        def suffix(u, z):
            kn = z.shape[0]
            sp = jnp.maximum(jnp.log2(1.0 + jnp.exp2(jnp.minimum(z, EXP2_CLAMP))), z)
            return _dot(upper[:kn, :kn], sp.astype(BF16))

        def weigh(u, z, c):
            (j, k0, kn, q0, qn, _), h = split(u)
            carry = carry_ref[h:h + 1, q0:q0 + qn]
            w = jnp.exp2(z - c - carry)
            acc_ref[h, :, q0:q0 + qn] += _dot(v_ref[0, h, j, :, k0:k0 + kn], w.astype(BF16))
            carry_ref[h:h + 1, q0:q0 + qn] = carry + c[0:1, :]

        next_rows = pl.ds(pl.multiple_of(j_next * tk, tk), tk)
        _pipelined_heads(score, suffix, weigh, n_units=len(blocks) * N_HEADS, stash_ref=stash_ref, stashed=stashed,
                         score_next=lambda h: raw_score(h, next_rows, 0, tq))

    step([(i, 0, tk, 0, tq, True)], jnp.maximum(i - 1, 0), False)

    def body(state):
        n, _ = state
        step([(i - 1 - n, 0, tk, 0, tq, False)], jnp.maximum(i - 2 - n, 0), True)
        return n + 1, jnp.min(carry_ref[...])

    lax.while_loop(lambda st: (st[0] < i) & (st[1] < SKIP_LOG2), body, (jnp.int32(0), jnp.float32(0.0)))
    _store_heads([acc_ref[h] for h in range(N_HEADS)], g_ref, gate_ref, o_ref)


def _mla_kernel(q_ref, k_ref, v_ref, gate_ref, g_ref, stats_ref, o_ref, acc_ref, m_ref):
    tq = q_ref.shape[1]
    tk = v_ref.shape[-1]
    ratio, half = tq // tk, tq // 2
    assert half <= tk and tk % half == 0
    i = pl.program_id(1)

    acc_ref[...] = jnp.zeros_like(acc_ref)
    m_ref[...] = jnp.full(m_ref.shape, -jnp.inf, F32)

    def step(blocks):
        def split(u):
            return blocks[u // N_HEADS], u % N_HEADS

        def score(u):
            (j, k0, kn, q0, qn, _), h = split(u)
            rows = pl.ds(pl.multiple_of(j * tk + k0, kn), kn)
            return _dot_nt(k_ref[0, h, rows, :], q_ref[0, q0:q0 + qn, h * LANES:(h + 1) * LANES])

        def update(u, s):
            (j, k0, kn, q0, qn, key_off), h = split(u)
            if key_off is not None:
                key = lax.broadcasted_iota(jnp.int32, (kn, qn), 0) + key_off
                qry = lax.broadcasted_iota(jnp.int32, (kn, qn), 1) + q0
                s = jnp.where((key // CHUNK) <= (qry // CHUNK), s, -jnp.inf)
            m = m_ref[h:h + 1, q0:q0 + qn]
            m_new = jnp.maximum(m, jnp.max(s, axis=0, keepdims=True))
            alpha = jnp.exp2(m - m_new)
            p = jnp.exp2(s - m_new)
            m_ref[h:h + 1, q0:q0 + qn] = m_new
            v1 = jnp.concatenate([v_ref[0, h, j, :, k0:k0 + kn], jnp.ones((SUM_ROWS, kn), BF16)], axis=0)
            acc_ref[h, :, q0:q0 + qn] = alpha * acc_ref[h, :, q0:q0 + qn] + _dot(v1, p.astype(BF16))

        _pipelined_heads(score, update, None, n_units=len(blocks) * N_HEADS)

    def fast_step(n):
        def score(h):
            rows = pl.ds(pl.multiple_of(n * tk, tk), tk)
            return _dot_nt(k_ref[0, h, rows, :], q_ref[0, :, h * LANES:(h + 1) * LANES])

        def update(h, s):
            p = jnp.exp2(s - m_ref[h:h + 1, :])
            v1 = jnp.concatenate([v_ref[0, h, n], jnp.ones((SUM_ROWS, tk), BF16)], axis=0)
            acc_ref[h] += _dot(v1, p.astype(BF16))

        _pipelined_heads(score, update, None)

    step([(i * ratio, 0, half, 0, tq, 0), (i * ratio + half // tk, half % tk, half, half, half, half)])

    q_sq = stats_ref[0, i * ratio, N_HEADS:, :]
    for c in range(1, ratio):
        q_sq = jnp.maximum(q_sq, stats_ref[0, i * ratio + c, N_HEADS:, :])

    def body(n, _):
        bound = jnp.sqrt(q_sq * stats_ref[0, n, :N_HEADS, :]) * NORM_SLACK
        excess = jnp.max(bound - jnp.min(m_ref[...], axis=1, keepdims=True))
        lax.cond(excess <= FAST_LOG2, lambda: fast_step(n), lambda: step([(n, 0, tk, 0, tq, None)]))
        return 0

    lax.fori_loop(0, i * ratio, body, 0)
    _store_heads([acc_ref[h, :HEAD_DIM] / acc_ref[h, HEAD_DIM:HEAD_DIM + 1] for h in range(N_HEADS)],
                 g_ref, gate_ref, o_ref)


def _attention(body, name, tq, q, k, v, gate, g, k_spec, acc_rows, stash, per_batch=()):
    B, S, _ = gate.shape
    tq = min(tq, S)
    assert S % tq == 0 and tq % v.shape[-1] == 0
    return pl.pallas_call(
        body,
        grid=(B, S // tq),
        in_specs=[pl.BlockSpec((1, tq, q.shape[-1]), lambda b, i: (b, i, 0)),
                  k_spec,
                  pl.BlockSpec((1,) + v.shape[1:], lambda b, i: (b, 0, 0, 0, 0)),
                  pl.BlockSpec((1, tq, D_GROUP), lambda b, i: (b, i, 0)),
                  pl.BlockSpec((1, D_GROUP), lambda b, i: (0, 0))]
        + [pl.BlockSpec((1,) + a.shape[1:], lambda b, i: (b, 0, 0, 0)) for a in per_batch],
        out_specs=pl.BlockSpec((1, tq, D_GROUP), lambda b, i: (b, i, 0)),
        out_shape=jax.ShapeDtypeStruct((B, S, D_GROUP), BF16),
        scratch_shapes=[pltpu.VMEM((N_HEADS, acc_rows, tq), F32), pltpu.VMEM((N_HEADS, tq), F32)]
        + [pltpu.VMEM((LOOKAHEAD, v.shape[-1], tq), F32)] * stash,
        compiler_params=pltpu.CompilerParams(dimension_semantics=("arbitrary", "arbitrary"),
                                             vmem_limit_bytes=VMEM_LIMIT),
        name=name,
    )(q, k, v, gate, g, *per_batch)


def _out_kernel(sb_ref, mla_ref, x_ref, p_ref, woa_ref, wob_ref, gpost_ref, wple_ref, gple_ref, wpg_ref, bpg_ref,
                o_ref):
    y = _dot(sb_ref[...], woa_ref[...]) + _dot(mla_ref[...], wob_ref[...])
    x1 = x_ref[...] + _rms(y, gpost_ref[...])
    ple = _rms(_dot(p_ref[...].astype(BF16), wple_ref[...]), gple_ref[...])
    gate = _sigmoid(_dot(x1.astype(BF16), wpg_ref[...]) + bpg_ref[...])
    o_ref[...] = x1 + ple * gate


def _output(sb_y, mla_y, x, p, woa, wob, gpost, wple, gple, wpg, bpg):
    B, S, D = x.shape
    tm = min(PROJ_ROWS, S)
    full = lambda a: pl.BlockSpec(a.shape, lambda b, t: (0,) * a.ndim)
    row = lambda w: pl.BlockSpec((None, tm, w), lambda b, t: (b, t, 0))
    weights = (woa, wob, gpost, wple, gple, wpg, bpg)
    return pl.pallas_call(
        _out_kernel,
        grid=(B, S // tm),
        in_specs=[row(D_GROUP), row(D_GROUP), row(D), row(p.shape[-1])] + [full(w) for w in weights],
        out_specs=row(D),
        out_shape=jax.ShapeDtypeStruct((B, S, D), F32),
        compiler_params=pltpu.CompilerParams(dimension_semantics=("arbitrary", "arbitrary"),
                                             vmem_limit_bytes=VMEM_LIMIT),
        name="proj_out",
    )(sb_y, mla_y, x, p, *weights)


def _rotate_half_cols(w):
    half = w.shape[-1] // 2
    return jnp.concatenate([-w[..., half:], w[..., :half]], axis=-1)


def _head_lanes(nope, rope):
    K = nope.shape[0]
    pad = jnp.zeros((K, N_HEADS, LANES - QK_NOPE_DIM - QK_ROPE_DIM), nope.dtype)
    return jnp.concatenate([nope, rope, pad], axis=-1).reshape(K, N_HEADS * LANES)


def _layer(x, p, pos, norm_pre_g, w_in, q_norm_g, w_uq, kv_norm_g, w_ukv, sb_out_norm_g, mla_out_norm_g, w_out,
           norm_post_g, w_ple, ple_norm_g, w_ple_gate, b_ple_gate):
    S = x.shape[1]
    row = lambda a: a.reshape(1, -1).astype(F32)
    c = np.cumsum([0, D_GROUP, D_GROUP, D_GROUP, D_GROUP, Q_LORA_RANK, KV_LORA_RANK, QK_ROPE_DIM, D_GROUP])
    wq, wk, wv, wg, wcq, wckv, wkr, wmg = (w_in[:, c[n]:c[n + 1]] for n in range(8))

    def rope_lanes(w):
        return jnp.pad(w, ((0, 0), (QK_NOPE_DIM, LANES - QK_NOPE_DIM - QK_ROPE_DIM)))

    wkr2 = jnp.concatenate([rope_lanes(wkr), rope_lanes(_rotate_half_cols(wkr))], axis=-1)
    uq = w_uq.reshape(Q_LORA_RANK, N_HEADS, QK_NOPE_DIM + QK_ROPE_DIM)
    uq_nope, uq_rope = uq[..., :QK_NOPE_DIM], uq[..., QK_NOPE_DIM:]
    wuq2 = jnp.concatenate([_head_lanes(uq_nope, uq_rope),
                            _head_lanes(jnp.zeros_like(uq_nope), _rotate_half_cols(uq_rope))], axis=-1)
    ukv = w_ukv.reshape(KV_LORA_RANK, N_HEADS, QK_NOPE_DIM + HEAD_DIM)
    uk, uv = ukv[..., :QK_NOPE_DIM], ukv[..., QK_NOPE_DIM:]
    wukv2 = jnp.concatenate([_head_lanes(uk, jnp.zeros((KV_LORA_RANK, N_HEADS, QK_ROPE_DIM), uk.dtype)),
                             uv.reshape(KV_LORA_RANK, D_GROUP)], axis=-1)
    half = QK_ROPE_DIM // 2
    freq = (ROPE_THETA ** (-jnp.arange(half, dtype=F32) / half)).reshape(half, 1)
    bf = lambda a: a.astype(BF16)

    sbq, sbk, sbv, sbg, mq, mk, mv, mg, stats = _projections(
        x, pos.astype(F32)[:, None, :], freq, row(norm_pre_g), bf(wq), bf(wk), bf(wv), bf(wg), bf(wcq), bf(wckv),
        bf(wkr2), bf(wmg), row(q_norm_g), bf(wuq2), row(kv_norm_g), bf(wukv2))
    sb_k_spec = pl.BlockSpec((1, S, D_GROUP), lambda b, i: (b, 0, 0))
    mla_k_spec = pl.BlockSpec((1, N_HEADS, S, LANES), lambda b, i: (b, 0, 0, 0))
    sb_y = _attention(_sb_kernel, "sb_attn", SB_TQ, sbq, sbk, sbv, sbg, row(sb_out_norm_g), sb_k_spec, HEAD_DIM, 1)
    mla_y = _attention(_mla_kernel, "mla_attn", MLA_TQ, mq, mk, mv, mg, row(mla_out_norm_g), mla_k_spec,
                       HEAD_DIM + SUM_ROWS, 0, (stats,))
    return _output(sb_y, mla_y, x, p, bf(w_out[:D_GROUP]), bf(w_out[D_GROUP:]), row(norm_post_g), bf(w_ple),
                   row(ple_norm_g), bf(w_ple_gate), row(b_ple_gate))


def kernel(x, p, positions, norm_pre_g, w_in, q_norm_g, w_uq, kv_norm_g, w_ukv, sb_out_norm_g, mla_out_norm_g, w_out,
           norm_post_g, w_ple, ple_norm_g, w_ple_gate, b_ple_gate):
    for i in range(p.shape[0]):
        x = _layer(x, p[i], positions, norm_pre_g[i], w_in[i], q_norm_g[i], w_uq[i], kv_norm_g[i], w_ukv[i],
                   sb_out_norm_g[i], mla_out_norm_g[i], w_out[i], norm_post_g[i], w_ple[i], ple_norm_g[i],
                   w_ple_gate[i], b_ple_gate[i])
    return x
```

```python
import jax
import jax.numpy as jnp
import numpy as np
from jax import lax
from jax.experimental import pallas as pl
from jax.experimental.pallas import tpu as pltpu

HEAD_DIM = 64
N_HEADS = 8
N_PAIRS = N_HEADS // 2
LANES = 128
QK_NOPE_DIM = 64
QK_ROPE_DIM = 32
Q_LORA_RANK = 256
KV_LORA_RANK = 128
CHUNK = 64
ROPE_THETA = 10000.0
EPS = 1e-6
D_GROUP = N_HEADS * HEAD_DIM
LOG2E = 1.4426950408889634

PROJ_ROWS = 1024
SB_TK = 256
SB_TQ = 256
MLA_TK = 512
MLA_TQ = 1024
DIAG_BANDS = 4
FAST_LOG2 = 60.0
NORM_SLACK = 1.02
SUM_ROWS = 16
LOOKAHEAD = 3
SKIP_LOG2 = 152.0
EXP2_CLAMP = 126.0
VMEM_LIMIT = 56 * 1024 * 1024

F32 = jnp.float32
BF16 = jnp.bfloat16


def _rms(x, g):
    return x * lax.rsqrt(jnp.mean(x * x, axis=-1, keepdims=True) + EPS) * g


def _dot(a, b):
    return jnp.dot(a, b, preferred_element_type=F32)


def _dot_nt(a, b):
    return lax.dot_general(a, b, (((1,), (1,)), ((), ())), preferred_element_type=F32)


def _sigmoid(x):
    return 1.0 / (1.0 + jnp.exp(-x))


def _proj_kernel(x_ref, pos_ref, freq_ref, gpre_ref, wq_ref, wk_ref, wv_ref, wg_ref, wcq_ref, wckv_ref,
                 wkr_ref, wmg_ref, gq_ref, wuq_ref, gkv_ref, wukv_ref,
                 sbq_ref, sbk_ref, sbv_ref, sbg_ref, mq_ref, mk_ref, mv_ref, mg_ref, stats_ref):
    h = _rms(x_ref[...], gpre_ref[...]).astype(BF16)
    n_stat = stats_ref.shape[1]

    def max_sq_norm(a):
        sq = jnp.sum(a * a, axis=-1, keepdims=True)
        rows = a.shape[0] // n_stat
        return [jnp.broadcast_to(jnp.max(sq[c * rows:(c + 1) * rows], axis=0, keepdims=True), (1, LANES))
                for c in range(n_stat)]

    def store_transposed(a, out_ref):
        tk = out_ref.shape[-1]
        for p in range(N_PAIRS):
            at = a[:, p * LANES:(p + 1) * LANES].T.astype(BF16)
            for c in range(out_ref.shape[2]):
                out_ref[0, 2 * p, c] = at[:HEAD_DIM, c * tk:(c + 1) * tk]
                out_ref[0, 2 * p + 1, c] = at[HEAD_DIM:, c * tk:(c + 1) * tk]

    sbq_ref[0] = (_dot(h, wq_ref[...]) * (HEAD_DIM ** -0.5 * LOG2E)).astype(BF16)
    sbk_ref[0] = _dot(h, wk_ref[...]).astype(BF16)
    store_transposed(_dot(h, wv_ref[...]), sbv_ref)
    sbg_ref[0] = _dot(h, wg_ref[...])
    mg_ref[0] = _dot(h, wmg_ref[...])

    ang_t = freq_ref[...] * pos_ref[0]
    n_freq = ang_t.shape[0]
    frow = lax.broadcasted_iota(jnp.int32, (n_freq, LANES), 0)
    rope_lane = lax.broadcasted_iota(jnp.int32, (n_freq, LANES), 1) - QK_NOPE_DIM
    spread = ((rope_lane == frow) | (rope_lane - n_freq == frow)).astype(BF16)

    def on_lanes(t):
        hi = t.astype(BF16)
        lo = (t - hi.astype(F32)).astype(BF16)
        tn = lambda a: lax.dot_general(a, spread, (((0,), (0,)), ((), ())), preferred_element_type=F32)
        return tn(hi) + tn(lo)

    nope_lanes = (lax.broadcasted_iota(jnp.int32, (1, LANES), 1) < QK_NOPE_DIM).astype(F32)
    cos, sin = on_lanes(jnp.cos(ang_t)) + nope_lanes, on_lanes(jnp.sin(ang_t))

    cq = _rms(_dot(h, wcq_ref[...]), gq_ref[...]).astype(BF16)
    q2 = _dot(cq, wuq_ref[...])
    scale = (QK_NOPE_DIM + QK_ROPE_DIM) ** -0.5 * LOG2E
    cos_q, sin_q = cos * scale, sin * scale
    half = N_HEADS * LANES
    q_norms, k_norms = [], []
    for hd in range(N_HEADS):
        qa = q2[:, hd * LANES:(hd + 1) * LANES]
        qb = q2[:, half + hd * LANES:half + (hd + 1) * LANES]
        qh = qa * cos_q + qb * sin_q
        q_norms.append(max_sq_norm(qh))
        mq_ref[0, :, hd * LANES:(hd + 1) * LANES] = qh.astype(BF16)

    ckv = _rms(_dot(h, wckv_ref[...]), gkv_ref[...]).astype(BF16)
    kv = _dot(ckv, wukv_ref[...])
    kr2 = _dot(h, wkr_ref[...])
    kr = kr2[:, :LANES] * cos + kr2[:, LANES:] * sin
    for hd in range(N_HEADS):
        kh = kv[:, hd * LANES:(hd + 1) * LANES] + kr
        k_norms.append(max_sq_norm(kh))
        mk_ref[0, hd] = kh.astype(BF16)
    store_transposed(kv[:, half:], mv_ref)
    for c in range(n_stat):
        stats_ref[0, c] = jnp.concatenate([n[c] for n in k_norms] + [n[c] for n in q_norms], axis=0)


def _projections(x, pos, freq, gpre, wq, wk, wv, wg, wcq, wckv, wkr, wmg, gq, wuq, gkv, wukv):
    B, S, D = x.shape
    tm = min(PROJ_ROWS, S)
    assert S % tm == 0
    nt = S // tm
    full = lambda a: pl.BlockSpec(a.shape, lambda b, t: (0,) * a.ndim)
    row = lambda w: pl.BlockSpec((1, tm, w), lambda b, t: (b, t, 0))
    head = pl.BlockSpec((1, N_HEADS, tm, LANES), lambda b, t: (b, 0, t, 0))

    def vt(tile):
        tk = min(tile, S)
        assert tm % tk == 0
        return (pl.BlockSpec((1, N_HEADS, tm // tk, HEAD_DIM, tk), lambda b, t: (b, 0, t, 0, 0)),
                jax.ShapeDtypeStruct((B, N_HEADS, S // tk, HEAD_DIM, tk), BF16))

    (sb_vt, sb_vt_shape), (mla_vt, mla_vt_shape) = vt(SB_TK), vt(MLA_TK)
    n_stat = tm // min(MLA_TK, S)
    stats = pl.BlockSpec((1, n_stat, 2 * N_HEADS, LANES), lambda b, t: (b, t, 0, 0))
    stats_shape = jax.ShapeDtypeStruct((B, nt * n_stat, 2 * N_HEADS, LANES), F32)
    weights = (freq, gpre, wq, wk, wv, wg, wcq, wckv, wkr, wmg, gq, wuq, gkv, wukv)
    return pl.pallas_call(
        _proj_kernel,
        grid=(B, nt),
        in_specs=[pl.BlockSpec((None, tm, D), lambda b, t: (b, t, 0)),
                  pl.BlockSpec((1, 1, tm), lambda b, t: (b, 0, t))] + [full(w) for w in weights],
        out_specs=[row(D_GROUP), row(D_GROUP), sb_vt, row(D_GROUP), row(N_HEADS * LANES), head, mla_vt, row(D_GROUP),
                   stats],
        out_shape=[jax.ShapeDtypeStruct((B, S, D_GROUP), BF16), jax.ShapeDtypeStruct((B, S, D_GROUP), BF16),
                   sb_vt_shape,
                   jax.ShapeDtypeStruct((B, S, D_GROUP), F32),
                   jax.ShapeDtypeStruct((B, S, N_HEADS * LANES), BF16),
                   jax.ShapeDtypeStruct((B, N_HEADS, S, LANES), BF16), mla_vt_shape,
                   jax.ShapeDtypeStruct((B, S, D_GROUP), F32), stats_shape],
        compiler_params=pltpu.CompilerParams(dimension_semantics=("arbitrary", "arbitrary"),
                                             vmem_limit_bytes=VMEM_LIMIT),
        name="proj_in",
    )(x, pos, *weights)


def _store_heads(o_t, g_ref, gate_ref, o_ref):
    for p in range(N_PAIRS):
        normed = [o * lax.rsqrt(jnp.mean(o * o, axis=0, keepdims=True) + EPS) for o in o_t[2 * p:2 * p + 2]]
        o_pair = jnp.concatenate(normed, axis=0).T
        cols = slice(p * LANES, (p + 1) * LANES)
        gate = gate_ref[0, :, cols]
        o_ref[0, :, cols] = (o_pair * g_ref[:, cols] * (gate * _sigmoid(gate))).astype(o_ref.dtype)


def _pipelined_heads(score, first, second, n_units=N_HEADS, stash_ref=None, stashed=False, score_next=None):
    s = [stash_ref[u] if stashed else score(u) for u in range(LOOKAHEAD)]
    mid = []
    for u in range(n_units):
        mid.append(first(u, s[u]))
        if u + LOOKAHEAD < n_units:
            s.append(score(u + LOOKAHEAD))
        elif stash_ref is not None:
            stash_ref[u + LOOKAHEAD - n_units] = score_next(u + LOOKAHEAD - n_units)
        if second is not None and u > 0:
            second(u - 1, s[u - 1], mid[u - 1])
    if second is not None:
        second(n_units - 1, s[-1], mid[-1])


def _sb_kernel(q_ref, k_ref, v_ref, gate_ref, g_ref, o_ref, acc_ref, carry_ref, stash_ref):
    tq = q_ref.shape[1]
    tk = v_ref.shape[-1]
    assert tq == tk
    i = pl.program_id(1)
    low_half = lax.broadcasted_iota(jnp.int32, (tq, LANES), 1) < HEAD_DIM
    qs = []
    for p in range(N_PAIRS):
        qp = q_ref[0, :, p * LANES:(p + 1) * LANES]
        qs += [jnp.where(low_half, qp, 0), jnp.where(low_half, 0, qp)]
    upper = (lax.broadcasted_iota(jnp.int32, (tk, tk), 1) >= lax.broadcasted_iota(jnp.int32, (tk, tk), 0)).astype(BF16)

    acc_ref[...] = jnp.zeros_like(acc_ref)
    carry_ref[...] = jnp.zeros_like(carry_ref)

    def raw_score(h, rows, q0, qn):
        pair = slice((h // 2) * LANES, (h // 2 + 1) * LANES)
        return _dot_nt(k_ref[0, rows, pair], qs[h][q0:q0 + qn])

    def step(blocks, j_next, stashed):
        def split(u):
            return blocks[u // N_HEADS], u % N_HEADS

        def score(u):
            (j, k0, kn, q0, qn, masked), h = split(u)
            z = raw_score(h, pl.ds(pl.multiple_of(j * tk + k0, kn), kn), q0, qn)
            if masked:
                past = (lax.broadcasted_iota(jnp.int32, (kn, qn), 0) + k0
                        < lax.broadcasted_iota(jnp.int32, (kn, qn), 1) + q0)
                z = jnp.where(past, z, -jnp.inf)
            return z

        def suffix(u, z):
            kn = z.shape[0]
            sp = jnp.maximum(jnp.log2(1.0 + jnp.exp2(jnp.minimum(z, EXP2_CLAMP))), z)
            return _dot(upper[:kn, :kn], sp.astype(BF16))

        def weigh(u, z, c):
            (j, k0, kn, q0, qn, _), h = split(u)
            carry = carry_ref[h:h + 1, q0:q0 + qn]
            w = jnp.exp2(z - c - carry)
            acc_ref[h, :, q0:q0 + qn] += _dot(v_ref[0, h, j, :, k0:k0 + kn], w.astype(BF16))
            carry_ref[h:h + 1, q0:q0 + qn] = carry + c[0:1, :]

        next_rows = pl.ds(pl.multiple_of(j_next * tk, tk), tk)
        _pipelined_heads(score, suffix, weigh, n_units=len(blocks) * N_HEADS, stash_ref=stash_ref, stashed=stashed,
                         score_next=lambda h: raw_score(h, next_rows, 0, tq))

    step([(i, 0, tk, 0, tq, True)], jnp.maximum(i - 1, 0), False)

    def body(state):
        n, _ = state
        step([(i - 1 - n, 0, tk, 0, tq, False)], jnp.maximum(i - 2 - n, 0), True)
        return n + 1, jnp.min(carry_ref[...])

    lax.while_loop(lambda st: (st[0] < i) & (st[1] < SKIP_LOG2), body, (jnp.int32(0), jnp.float32(0.0)))
    _store_heads([acc_ref[h] for h in range(N_HEADS)], g_ref, gate_ref, o_ref)


def _mla_kernel(q_ref, k_ref, v_ref, gate_ref, g_ref, stats_ref, o_ref, acc_ref, m_ref):
    tq = q_ref.shape[1]
    tk = v_ref.shape[-1]
    ratio = tq // tk
    assert tk % (tq // DIAG_BANDS) == 0
    i = pl.program_id(1)

    acc_ref[...] = jnp.zeros_like(acc_ref)

    def unit(blocks, u):
        return blocks[u // N_HEADS], u % N_HEADS

    def score(blocks, u):
        (j, k0, kn, q0, qn, _), h = unit(blocks, u)
        rows = pl.ds(pl.multiple_of(j * tk + k0, kn), kn)
        return _dot_nt(k_ref[0, h, rows, :], q_ref[0, q0:q0 + qn, h * LANES:(h + 1) * LANES])

    def visible(kn, qn, q0, key_off):
        key = lax.broadcasted_iota(jnp.int32, (kn, qn), 0) + key_off
        qry = lax.broadcasted_iota(jnp.int32, (kn, qn), 1) + q0
        return (key // CHUNK) <= (qry // CHUNK)

    def values_and_ones(h, j, k0, kn):
        return jnp.concatenate([v_ref[0, h, j, :, k0:k0 + kn], jnp.ones((SUM_ROWS, kn), BF16)], axis=0)

    def step(blocks):
        def update(u, s):
            (j, k0, kn, q0, qn, key_off), h = unit(blocks, u)
            if key_off is not None:
                s = jnp.where(visible(kn, qn, q0, key_off), s, -jnp.inf)
            m = m_ref[h:h + 1, q0:q0 + qn]
            m_new = jnp.maximum(m, jnp.max(s, axis=0, keepdims=True))
            alpha = jnp.exp2(m - m_new)
            p = jnp.exp2(s - m_new)
            m_ref[h:h + 1, q0:q0 + qn] = m_new
            acc_ref[h, :, q0:q0 + qn] = (alpha * acc_ref[h, :, q0:q0 + qn]
                                         + _dot(values_and_ones(h, j, k0, kn), p.astype(BF16)))

        _pipelined_heads(lambda u: score(blocks, u), update, None, n_units=len(blocks) * N_HEADS)

    def fast_step(blocks):
        def update(u, s):
            (j, k0, kn, q0, qn, key_off), h = unit(blocks, u)
            p = jnp.exp2(s - m_ref[h:h + 1, q0:q0 + qn])
            if key_off is not None:
                p = jnp.where(visible(kn, qn, q0, key_off), p, 0.0)
            acc_ref[h, :, q0:q0 + qn] += _dot(values_and_ones(h, j, k0, kn), p.astype(BF16))

        _pipelined_heads(lambda u: score(blocks, u), update, None, n_units=len(blocks) * N_HEADS)

    def sq_norms(first_tile, rows):
        sq = stats_ref[0, first_tile, rows, :]
        for c in range(1, ratio):
            sq = jnp.maximum(sq, stats_ref[0, first_tile + c, rows, :])
        return sq

    q_sq = sq_norms(i * ratio, slice(N_HEADS, 2 * N_HEADS))

    band = tq // DIAG_BANDS
    diagonal = [(i * ratio + (b * band) // tk, (b * band) % tk, band, b * band, tq - b * band, b * band)
                for b in range(DIAG_BANDS)]
    bound = jnp.sqrt(q_sq * sq_norms(i * ratio, slice(0, N_HEADS))) * NORM_SLACK
    m_ref[...] = jnp.tile(bound - FAST_LOG2, (1, tq // LANES))
    fast_step(diagonal)
    denom = acc_ref[:, HEAD_DIM, :]

    def renormalise():
        for h in range(N_HEADS):
            acc_ref[h] = acc_ref[h] / denom[h:h + 1, :]
        m_ref[...] = m_ref[...] + jnp.log2(denom)

    def redo():
        acc_ref[...] = jnp.zeros_like(acc_ref)
        m_ref[...] = jnp.full(m_ref.shape, -jnp.inf, F32)
        step(diagonal)

    lax.cond(jnp.min(denom) >= 2.0 ** -FAST_LOG2, renormalise, redo)

    def body(n, _):
        tile = [(n, 0, tk, 0, tq, None)]
        excess = jnp.max(jnp.sqrt(q_sq * stats_ref[0, n, :N_HEADS, :]) * NORM_SLACK
                         - jnp.min(m_ref[...], axis=1, keepdims=True))
        lax.cond(excess <= FAST_LOG2, lambda: fast_step(tile), lambda: step(tile))
        return 0

    lax.fori_loop(0, i * ratio, body, 0)
    _store_heads([acc_ref[h, :HEAD_DIM] / acc_ref[h, HEAD_DIM:HEAD_DIM + 1] for h in range(N_HEADS)],
                 g_ref, gate_ref, o_ref)


def _attention(body, name, tq, q, k, v, gate, g, k_spec, acc_rows, stash, per_batch=()):
    B, S, _ = gate.shape
    tq = min(tq, S)
    assert S % tq == 0 and tq % v.shape[-1] == 0
    return pl.pallas_call(
        body,
        grid=(B, S // tq),
        in_specs=[pl.BlockSpec((1, tq, q.shape[-1]), lambda b, i: (b, i, 0)),
                  k_spec,
                  pl.BlockSpec((1,) + v.shape[1:], lambda b, i: (b, 0, 0, 0, 0)),
                  pl.BlockSpec((1, tq, D_GROUP), lambda b, i: (b, i, 0)),
                  pl.BlockSpec((1, D_GROUP), lambda b, i: (0, 0))]
        + [pl.BlockSpec((1,) + a.shape[1:], lambda b, i: (b, 0, 0, 0)) for a in per_batch],
        out_specs=pl.BlockSpec((1, tq, D_GROUP), lambda b, i: (b, i, 0)),
        out_shape=jax.ShapeDtypeStruct((B, S, D_GROUP), BF16),
        scratch_shapes=[pltpu.VMEM((N_HEADS, acc_rows, tq), F32), pltpu.VMEM((N_HEADS, tq), F32)]
        + [pltpu.VMEM((LOOKAHEAD, v.shape[-1], tq), F32)] * stash,
        compiler_params=pltpu.CompilerParams(dimension_semantics=("arbitrary", "arbitrary"),
                                             vmem_limit_bytes=VMEM_LIMIT),
        name=name,
    )(q, k, v, gate, g, *per_batch)


def _out_kernel(sb_ref, mla_ref, x_ref, p_ref, woa_ref, wob_ref, gpost_ref, wple_ref, gple_ref, wpg_ref, bpg_ref,
                o_ref):
    y = _dot(sb_ref[...], woa_ref[...]) + _dot(mla_ref[...], wob_ref[...])
    x1 = x_ref[...] + _rms(y, gpost_ref[...])
    ple = _rms(_dot(p_ref[...].astype(BF16), wple_ref[...]), gple_ref[...])
    gate = _sigmoid(_dot(x1.astype(BF16), wpg_ref[...]) + bpg_ref[...])
    o_ref[...] = x1 + ple * gate


def _output(sb_y, mla_y, x, p, woa, wob, gpost, wple, gple, wpg, bpg):
    B, S, D = x.shape
    tm = min(PROJ_ROWS, S)
    full = lambda a: pl.BlockSpec(a.shape, lambda b, t: (0,) * a.ndim)
    row = lambda w: pl.BlockSpec((None, tm, w), lambda b, t: (b, t, 0))
    weights = (woa, wob, gpost, wple, gple, wpg, bpg)
    return pl.pallas_call(
        _out_kernel,
        grid=(B, S // tm),
        in_specs=[row(D_GROUP), row(D_GROUP), row(D), row(p.shape[-1])] + [full(w) for w in weights],
        out_specs=row(D),
        out_shape=jax.ShapeDtypeStruct((B, S, D), F32),
        compiler_params=pltpu.CompilerParams(dimension_semantics=("arbitrary", "arbitrary"),
                                             vmem_limit_bytes=VMEM_LIMIT),
        name="proj_out",
    )(sb_y, mla_y, x, p, *weights)


def _rotate_half_cols(w):
    half = w.shape[-1] // 2
    return jnp.concatenate([-w[..., half:], w[..., :half]], axis=-1)


def _head_lanes(nope, rope):
    K = nope.shape[0]
    pad = jnp.zeros((K, N_HEADS, LANES - QK_NOPE_DIM - QK_ROPE_DIM), nope.dtype)
    return jnp.concatenate([nope, rope, pad], axis=-1).reshape(K, N_HEADS * LANES)


def _layer(x, p, pos, norm_pre_g, w_in, q_norm_g, w_uq, kv_norm_g, w_ukv, sb_out_norm_g, mla_out_norm_g, w_out,
           norm_post_g, w_ple, ple_norm_g, w_ple_gate, b_ple_gate):
    S = x.shape[1]
    row = lambda a: a.reshape(1, -1).astype(F32)
    c = np.cumsum([0, D_GROUP, D_GROUP, D_GROUP, D_GROUP, Q_LORA_RANK, KV_LORA_RANK, QK_ROPE_DIM, D_GROUP])
    wq, wk, wv, wg, wcq, wckv, wkr, wmg = (w_in[:, c[n]:c[n + 1]] for n in range(8))

    def rope_lanes(w):
        return jnp.pad(w, ((0, 0), (QK_NOPE_DIM, LANES - QK_NOPE_DIM - QK_ROPE_DIM)))

    wkr2 = jnp.concatenate([rope_lanes(wkr), rope_lanes(_rotate_half_cols(wkr))], axis=-1)
    uq = w_uq.reshape(Q_LORA_RANK, N_HEADS, QK_NOPE_DIM + QK_ROPE_DIM)
    uq_nope, uq_rope = uq[..., :QK_NOPE_DIM], uq[..., QK_NOPE_DIM:]
    wuq2 = jnp.concatenate([_head_lanes(uq_nope, uq_rope),
                            _head_lanes(jnp.zeros_like(uq_nope), _rotate_half_cols(uq_rope))], axis=-1)
    ukv = w_ukv.reshape(KV_LORA_RANK, N_HEADS, QK_NOPE_DIM + HEAD_DIM)
    uk, uv = ukv[..., :QK_NOPE_DIM], ukv[..., QK_NOPE_DIM:]
    wukv2 = jnp.concatenate([_head_lanes(uk, jnp.zeros((KV_LORA_RANK, N_HEADS, QK_ROPE_DIM), uk.dtype)),
                             uv.reshape(KV_LORA_RANK, D_GROUP)], axis=-1)
    half = QK_ROPE_DIM // 2
    freq = (ROPE_THETA ** (-jnp.arange(half, dtype=F32) / half)).reshape(half, 1)
    bf = lambda a: a.astype(BF16)

    sbq, sbk, sbv, sbg, mq, mk, mv, mg, stats = _projections(
        x, pos.astype(F32)[:, None, :], freq, row(norm_pre_g), bf(wq), bf(wk), bf(wv), bf(wg), bf(wcq), bf(wckv),
        bf(wkr2), bf(wmg), row(q_norm_g), bf(wuq2), row(kv_norm_g), bf(wukv2))
    sb_k_spec = pl.BlockSpec((1, S, D_GROUP), lambda b, i: (b, 0, 0))
    mla_k_spec = pl.BlockSpec((1, N_HEADS, S, LANES), lambda b, i: (b, 0, 0, 0))
    sb_y = _attention(_sb_kernel, "sb_attn", SB_TQ, sbq, sbk, sbv, sbg, row(sb_out_norm_g), sb_k_spec, HEAD_DIM, 1)
    mla_y = _attention(_mla_kernel, "mla_attn", MLA_TQ, mq, mk, mv, mg, row(mla_out_norm_g), mla_k_spec,
                       HEAD_DIM + SUM_ROWS, 0, (stats,))
    return _output(sb_y, mla_y, x, p, bf(w_out[:D_GROUP]), bf(w_out[D_GROUP:]), row(norm_post_g), bf(w_ple),
                   row(ple_norm_g), bf(w_ple_gate), row(b_ple_gate))


def kernel(x, p, positions, norm_pre_g, w_in, q_norm_g, w_uq, kv_norm_g, w_ukv, sb_out_norm_g, mla_out_norm_g, w_out,
           norm_post_g, w_ple, ple_norm_g, w_ple_gate, b_ple_gate):
    for i in range(p.shape[0]):
        x = _layer(x, p[i], positions, norm_pre_g[i], w_in[i], q_norm_g[i], w_uq[i], kv_norm_g[i], w_ukv[i],
                   sb_out_norm_g[i], mla_out_norm_g[i], w_out[i], norm_post_g[i], w_ple[i], ple_norm_g[i],
                   w_ple_gate[i], b_ple_gate[i])
    return x
```

```python
import jax
import jax.numpy as jnp
import numpy as np
from jax import lax
from jax.experimental import pallas as pl
from jax.experimental.pallas import tpu as pltpu

HEAD_DIM = 64
N_HEADS = 8
N_PAIRS = N_HEADS // 2
LANES = 128
QK_NOPE_DIM = 64
QK_ROPE_DIM = 32
Q_LORA_RANK = 256
KV_LORA_RANK = 128
CHUNK = 64
ROPE_THETA = 10000.0
EPS = 1e-6
D_GROUP = N_HEADS * HEAD_DIM
LOG2E = 1.4426950408889634

PROJ_ROWS = 1024
SB_TK = 256
SB_TQ = 256
MLA_TK = 1024
MLA_TQ = 1024
DIAG_BANDS = 4
FAST_LOG2 = 60.0
NORM_SLACK = 1.02
SUM_ROWS = 16
LOOKAHEAD = 3
SKIP_LOG2 = 152.0
EXP2_CLAMP = 126.0
VMEM_LIMIT = 56 * 1024 * 1024

F32 = jnp.float32
BF16 = jnp.bfloat16


def _rms(x, g):
    return x * lax.rsqrt(jnp.mean(x * x, axis=-1, keepdims=True) + EPS) * g


def _dot(a, b):
    return jnp.dot(a, b, preferred_element_type=F32)


def _dot_nt(a, b):
    return lax.dot_general(a, b, (((1,), (1,)), ((), ())), preferred_element_type=F32)


def _sigmoid(x):
    return 1.0 / (1.0 + jnp.exp(-x))


def _proj_kernel(x_ref, pos_ref, freq_ref, gpre_ref, wq_ref, wk_ref, wv_ref, wg_ref, wcq_ref, wckv_ref,
                 wkr_ref, wmg_ref, gq_ref, wuq_ref, gkv_ref, wukv_ref,
                 sbq_ref, sbk_ref, sbv_ref, sbg_ref, mq_ref, mk_ref, mv_ref, mg_ref, stats_ref):
    h = _rms(x_ref[...], gpre_ref[...]).astype(BF16)
    n_stat = stats_ref.shape[1]

    def max_sq_norm(a):
        sq = jnp.sum(a * a, axis=-1, keepdims=True)
        rows = a.shape[0] // n_stat
        return [jnp.broadcast_to(jnp.max(sq[c * rows:(c + 1) * rows], axis=0, keepdims=True), (1, LANES))
                for c in range(n_stat)]

    def store_transposed(a, out_ref):
        tk = out_ref.shape[-1]
        for p in range(N_PAIRS):
            at = a[:, p * LANES:(p + 1) * LANES].T.astype(BF16)
            for c in range(out_ref.shape[2]):
                out_ref[0, 2 * p, c] = at[:HEAD_DIM, c * tk:(c + 1) * tk]
                out_ref[0, 2 * p + 1, c] = at[HEAD_DIM:, c * tk:(c + 1) * tk]

    sbq_ref[0] = (_dot(h, wq_ref[...]) * (HEAD_DIM ** -0.5 * LOG2E)).astype(BF16)
    sbk_ref[0] = _dot(h, wk_ref[...]).astype(BF16)
    store_transposed(_dot(h, wv_ref[...]), sbv_ref)
    sbg_ref[0] = _dot(h, wg_ref[...])
    mg_ref[0] = _dot(h, wmg_ref[...])

    ang_t = freq_ref[...] * pos_ref[0]
    n_freq = ang_t.shape[0]
    frow = lax.broadcasted_iota(jnp.int32, (n_freq, LANES), 0)
    rope_lane = lax.broadcasted_iota(jnp.int32, (n_freq, LANES), 1) - QK_NOPE_DIM
    spread = ((rope_lane == frow) | (rope_lane - n_freq == frow)).astype(BF16)

    def on_lanes(t):
        hi = t.astype(BF16)
        lo = (t - hi.astype(F32)).astype(BF16)
        tn = lambda a: lax.dot_general(a, spread, (((0,), (0,)), ((), ())), preferred_element_type=F32)
        return tn(hi) + tn(lo)

    nope_lanes = (lax.broadcasted_iota(jnp.int32, (1, LANES), 1) < QK_NOPE_DIM).astype(F32)
    cos, sin = on_lanes(jnp.cos(ang_t)) + nope_lanes, on_lanes(jnp.sin(ang_t))

    cq = _rms(_dot(h, wcq_ref[...]), gq_ref[...]).astype(BF16)
    q2 = _dot(cq, wuq_ref[...])
    scale = (QK_NOPE_DIM + QK_ROPE_DIM) ** -0.5 * LOG2E
    cos_q, sin_q = cos * scale, sin * scale
    half = N_HEADS * LANES
    q_norms, k_norms = [], []
    for hd in range(N_HEADS):
        qa = q2[:, hd * LANES:(hd + 1) * LANES]
        qb = q2[:, half + hd * LANES:half + (hd + 1) * LANES]
        qh = qa * cos_q + qb * sin_q
        q_norms.append(max_sq_norm(qh))
        mq_ref[0, :, hd * LANES:(hd + 1) * LANES] = qh.astype(BF16)

    ckv = _rms(_dot(h, wckv_ref[...]), gkv_ref[...]).astype(BF16)
    kv = _dot(ckv, wukv_ref[...])
    kr2 = _dot(h, wkr_ref[...])
    kr = kr2[:, :LANES] * cos + kr2[:, LANES:] * sin
    for hd in range(N_HEADS):
        kh = kv[:, hd * LANES:(hd + 1) * LANES] + kr
        k_norms.append(max_sq_norm(kh))
        mk_ref[0, hd] = kh.astype(BF16)
    store_transposed(kv[:, half:], mv_ref)
    for c in range(n_stat):
        stats_ref[0, c] = jnp.concatenate([n[c] for n in k_norms] + [n[c] for n in q_norms], axis=0)


def _projections(x, pos, freq, gpre, wq, wk, wv, wg, wcq, wckv, wkr, wmg, gq, wuq, gkv, wukv):
    B, S, D = x.shape
    tm = min(PROJ_ROWS, S)
    assert S % tm == 0
    nt = S // tm
    full = lambda a: pl.BlockSpec(a.shape, lambda b, t: (0,) * a.ndim)
    row = lambda w: pl.BlockSpec((1, tm, w), lambda b, t: (b, t, 0))
    head = pl.BlockSpec((1, N_HEADS, tm, LANES), lambda b, t: (b, 0, t, 0))

    def vt(tile):
        tk = min(tile, S)
        assert tm % tk == 0
        return (pl.BlockSpec((1, N_HEADS, tm // tk, HEAD_DIM, tk), lambda b, t: (b, 0, t, 0, 0)),
                jax.ShapeDtypeStruct((B, N_HEADS, S // tk, HEAD_DIM, tk), BF16))

    (sb_vt, sb_vt_shape), (mla_vt, mla_vt_shape) = vt(SB_TK), vt(MLA_TK)
    n_stat = tm // min(MLA_TK, S)
    stats = pl.BlockSpec((1, n_stat, 2 * N_HEADS, LANES), lambda b, t: (b, t, 0, 0))
    stats_shape = jax.ShapeDtypeStruct((B, nt * n_stat, 2 * N_HEADS, LANES), F32)
    weights = (freq, gpre, wq, wk, wv, wg, wcq, wckv, wkr, wmg, gq, wuq, gkv, wukv)
    return pl.pallas_call(
        _proj_kernel,
        grid=(B, nt),
        in_specs=[pl.BlockSpec((None, tm, D), lambda b, t: (b, t, 0)),
                  pl.BlockSpec((1, 1, tm), lambda b, t: (b, 0, t))] + [full(w) for w in weights],
        out_specs=[row(D_GROUP), row(D_GROUP), sb_vt, row(D_GROUP), row(N_HEADS * LANES), head, mla_vt, row(D_GROUP),
                   stats],
        out_shape=[jax.ShapeDtypeStruct((B, S, D_GROUP), BF16), jax.ShapeDtypeStruct((B, S, D_GROUP), BF16),
                   sb_vt_shape,
                   jax.ShapeDtypeStruct((B, S, D_GROUP), F32),
                   jax.ShapeDtypeStruct((B, S, N_HEADS * LANES), BF16),
                   jax.ShapeDtypeStruct((B, N_HEADS, S, LANES), BF16), mla_vt_shape,
                   jax.ShapeDtypeStruct((B, S, D_GROUP), F32), stats_shape],
        compiler_params=pltpu.CompilerParams(dimension_semantics=("arbitrary", "arbitrary"),
                                             vmem_limit_bytes=VMEM_LIMIT),
        name="proj_in",
    )(x, pos, *weights)


def _store_heads(o_t, g_ref, gate_ref, o_ref):
    for p in range(N_PAIRS):
        normed = [o * lax.rsqrt(jnp.mean(o * o, axis=0, keepdims=True) + EPS) for o in o_t[2 * p:2 * p + 2]]
        o_pair = jnp.concatenate(normed, axis=0).T
        cols = slice(p * LANES, (p + 1) * LANES)
        gate = gate_ref[0, :, cols]
        o_ref[0, :, cols] = (o_pair * g_ref[:, cols] * (gate * _sigmoid(gate))).astype(o_ref.dtype)


def _pipelined_heads(score, first, second, n_units=N_HEADS, stash_ref=None, stashed=False, score_next=None):
    s = [stash_ref[u] if stashed else score(u) for u in range(LOOKAHEAD)]
    mid = []
    for u in range(n_units):
        mid.append(first(u, s[u]))
        if u + LOOKAHEAD < n_units:
            s.append(score(u + LOOKAHEAD))
        elif stash_ref is not None:
            stash_ref[u + LOOKAHEAD - n_units] = score_next(u + LOOKAHEAD - n_units)
        if second is not None and u > 0:
            second(u - 1, s[u - 1], mid[u - 1])
    if second is not None:
        second(n_units - 1, s[-1], mid[-1])


def _sb_kernel(q_ref, k_ref, v_ref, gate_ref, g_ref, o_ref, acc_ref, carry_ref, stash_ref):
    tq = q_ref.shape[1]
    tk = v_ref.shape[-1]
    assert tq == tk
    i = pl.program_id(1)
    low_half = lax.broadcasted_iota(jnp.int32, (tq, LANES), 1) < HEAD_DIM
    qs = []
    for p in range(N_PAIRS):
        qp = q_ref[0, :, p * LANES:(p + 1) * LANES]
        qs += [jnp.where(low_half, qp, 0), jnp.where(low_half, 0, qp)]
    upper = (lax.broadcasted_iota(jnp.int32, (tk, tk), 1) >= lax.broadcasted_iota(jnp.int32, (tk, tk), 0)).astype(BF16)

    acc_ref[...] = jnp.zeros_like(acc_ref)
    carry_ref[...] = jnp.zeros_like(carry_ref)

    def raw_score(h, rows, q0, qn):
        pair = slice((h // 2) * LANES, (h // 2 + 1) * LANES)
        return _dot_nt(k_ref[0, rows, pair], qs[h][q0:q0 + qn])

    def step(blocks, j_next, stashed):
        def split(u):
            return blocks[u // N_HEADS], u % N_HEADS

        def score(u):
            (j, k0, kn, q0, qn, masked), h = split(u)
            z = raw_score(h, pl.ds(pl.multiple_of(j * tk + k0, kn), kn), q0, qn)
            if masked:
                past = (lax.broadcasted_iota(jnp.int32, (kn, qn), 0) + k0
                        < lax.broadcasted_iota(jnp.int32, (kn, qn), 1) + q0)
                z = jnp.where(past, z, -jnp.inf)
            return z

        def suffix(u, z):
            kn = z.shape[0]
            sp = jnp.maximum(jnp.log2(1.0 + jnp.exp2(jnp.minimum(z, EXP2_CLAMP))), z)
            return _dot(upper[:kn, :kn], sp.astype(BF16))

        def weigh(u, z, c):
            (j, k0, kn, q0, qn, _), h = split(u)
            carry = carry_ref[h:h + 1, q0:q0 + qn]
            w = jnp.exp2(z - c - carry)
            acc_ref[h, :, q0:q0 + qn] += _dot(v_ref[0, h, j, :, k0:k0 + kn], w.astype(BF16))
            carry_ref[h:h + 1, q0:q0 + qn] = carry + c[0:1, :]

        next_rows = pl.ds(pl.multiple_of(j_next * tk, tk), tk)
        _pipelined_heads(score, suffix, weigh, n_units=len(blocks) * N_HEADS, stash_ref=stash_ref, stashed=stashed,
                         score_next=lambda h: raw_score(h, next_rows, 0, tq))

    step([(i, 0, tk, 0, tq, True)], jnp.maximum(i - 1, 0), False)

    def body(state):
        n, _ = state
        step([(i - 1 - n, 0, tk, 0, tq, False)], jnp.maximum(i - 2 - n, 0), True)
        return n + 1, jnp.min(carry_ref[...])

    lax.while_loop(lambda st: (st[0] < i) & (st[1] < SKIP_LOG2), body, (jnp.int32(0), jnp.float32(0.0)))
    _store_heads([acc_ref[h] for h in range(N_HEADS)], g_ref, gate_ref, o_ref)


def _mla_kernel(q_ref, k_ref, v_ref, gate_ref, g_ref, stats_ref, o_ref, acc_ref, m_ref):
    tq = q_ref.shape[1]
    tk = v_ref.shape[-1]
    ratio = tq // tk
    assert tk % (tq // DIAG_BANDS) == 0
    i = pl.program_id(1)

    acc_ref[...] = jnp.zeros_like(acc_ref)

    def unit(blocks, u):
        return blocks[u // N_HEADS], u % N_HEADS

    def score(blocks, u):
        (j, k0, kn, q0, qn, _), h = unit(blocks, u)
        rows = pl.ds(pl.multiple_of(j * tk + k0, kn), kn)
        return _dot_nt(k_ref[0, h, rows, :], q_ref[0, q0:q0 + qn, h * LANES:(h + 1) * LANES])

    def visible(kn, qn, q0, key_off):
        key = lax.broadcasted_iota(jnp.int32, (kn, qn), 0) + key_off
        qry = lax.broadcasted_iota(jnp.int32, (kn, qn), 1) + q0
        return (key // CHUNK) <= (qry // CHUNK)

    def values_and_ones(h, j, k0, kn):
        return jnp.concatenate([v_ref[0, h, j, :, k0:k0 + kn], jnp.ones((SUM_ROWS, kn), BF16)], axis=0)

    def step(blocks):
        def update(u, s):
            (j, k0, kn, q0, qn, key_off), h = unit(blocks, u)
            if key_off is not None:
                s = jnp.where(visible(kn, qn, q0, key_off), s, -jnp.inf)
            m = m_ref[h:h + 1, q0:q0 + qn]
            m_new = jnp.maximum(m, jnp.max(s, axis=0, keepdims=True))
            alpha = jnp.exp2(m - m_new)
            p = jnp.exp2(s - m_new)
            m_ref[h:h + 1, q0:q0 + qn] = m_new
            acc_ref[h, :, q0:q0 + qn] = (alpha * acc_ref[h, :, q0:q0 + qn]
                                         + _dot(values_and_ones(h, j, k0, kn), p.astype(BF16)))

        _pipelined_heads(lambda u: score(blocks, u), update, None, n_units=len(blocks) * N_HEADS)

    def fast_step(blocks):
        def update(u, s):
            (j, k0, kn, q0, qn, key_off), h = unit(blocks, u)
            p = jnp.exp2(s - m_ref[h:h + 1, q0:q0 + qn])
            if key_off is not None:
                p = jnp.where(visible(kn, qn, q0, key_off), p, 0.0)
            acc_ref[h, :, q0:q0 + qn] += _dot(values_and_ones(h, j, k0, kn), p.astype(BF16))

        _pipelined_heads(lambda u: score(blocks, u), update, None, n_units=len(blocks) * N_HEADS)

    def sq_norms(first_tile, rows):
        sq = stats_ref[0, first_tile, rows, :]
        for c in range(1, ratio):
            sq = jnp.maximum(sq, stats_ref[0, first_tile + c, rows, :])
        return sq

    q_sq = sq_norms(i * ratio, slice(N_HEADS, 2 * N_HEADS))

    band = tq // DIAG_BANDS
    diagonal = [(i * ratio + (b * band) // tk, (b * band) % tk, band, b * band, tq - b * band, b * band)
                for b in range(DIAG_BANDS)]
    bound = jnp.sqrt(q_sq * sq_norms(i * ratio, slice(0, N_HEADS))) * NORM_SLACK
    m_ref[...] = jnp.tile(bound - FAST_LOG2, (1, tq // LANES))
    fast_step(diagonal)
    denom = acc_ref[:, HEAD_DIM, :]

    def renormalise():
        for h in range(N_HEADS):
            acc_ref[h] = acc_ref[h] / denom[h:h + 1, :]
        m_ref[...] = m_ref[...] + jnp.log2(denom)

    def redo():
        acc_ref[...] = jnp.zeros_like(acc_ref)
        m_ref[...] = jnp.full(m_ref.shape, -jnp.inf, F32)
        step(diagonal)

    lax.cond(jnp.min(denom) >= 2.0 ** -FAST_LOG2, renormalise, redo)

    def body(n, _):
        tile = [(n, 0, tk, 0, tq, None)]
        excess = jnp.max(jnp.sqrt(q_sq * stats_ref[0, n, :N_HEADS, :]) * NORM_SLACK
                         - jnp.min(m_ref[...], axis=1, keepdims=True))
        lax.cond(excess <= FAST_LOG2, lambda: fast_step(tile), lambda: step(tile))
        return 0

    lax.fori_loop(0, i * ratio, body, 0)
    _store_heads([acc_ref[h, :HEAD_DIM] / acc_ref[h, HEAD_DIM:HEAD_DIM + 1] for h in range(N_HEADS)],
                 g_ref, gate_ref, o_ref)


def _attention(body, name, tq, q, k, v, gate, g, k_spec, acc_rows, stash, per_batch=()):
    B, S, _ = gate.shape
    tq = min(tq, S)
    assert S % tq == 0 and tq % v.shape[-1] == 0
    return pl.pallas_call(
        body,
        grid=(B, S // tq),
        in_specs=[pl.BlockSpec((1, tq, q.shape[-1]), lambda b, i: (b, i, 0)),
                  k_spec,
                  pl.BlockSpec((1,) + v.shape[1:], lambda b, i: (b, 0, 0, 0, 0)),
                  pl.BlockSpec((1, tq, D_GROUP), lambda b, i: (b, i, 0)),
                  pl.BlockSpec((1, D_GROUP), lambda b, i: (0, 0))]
        + [pl.BlockSpec((1,) + a.shape[1:], lambda b, i: (b, 0, 0, 0)) for a in per_batch],
        out_specs=pl.BlockSpec((1, tq, D_GROUP), lambda b, i: (b, i, 0)),
        out_shape=jax.ShapeDtypeStruct((B, S, D_GROUP), BF16),
        scratch_shapes=[pltpu.VMEM((N_HEADS, acc_rows, tq), F32), pltpu.VMEM((N_HEADS, tq), F32)]
        + [pltpu.VMEM((LOOKAHEAD, v.shape[-1], tq), F32)] * stash,
        compiler_params=pltpu.CompilerParams(dimension_semantics=("arbitrary", "arbitrary"),
                                             vmem_limit_bytes=VMEM_LIMIT),
        name=name,
    )(q, k, v, gate, g, *per_batch)


def _out_kernel(sb_ref, mla_ref, x_ref, p_ref, woa_ref, wob_ref, gpost_ref, wple_ref, gple_ref, wpg_ref, bpg_ref,
                o_ref):
    y = _dot(sb_ref[...], woa_ref[...]) + _dot(mla_ref[...], wob_ref[...])
    x1 = x_ref[...] + _rms(y, gpost_ref[...])
    ple = _rms(_dot(p_ref[...].astype(BF16), wple_ref[...]), gple_ref[...])
    gate = _sigmoid(_dot(x1.astype(BF16), wpg_ref[...]) + bpg_ref[...])
    o_ref[...] = x1 + ple * gate


def _output(sb_y, mla_y, x, p, woa, wob, gpost, wple, gple, wpg, bpg):
    B, S, D = x.shape
    tm = min(PROJ_ROWS, S)
    full = lambda a: pl.BlockSpec(a.shape, lambda b, t: (0,) * a.ndim)
    row = lambda w: pl.BlockSpec((None, tm, w), lambda b, t: (b, t, 0))
    weights = (woa, wob, gpost, wple, gple, wpg, bpg)
    return pl.pallas_call(
        _out_kernel,
        grid=(B, S // tm),
        in_specs=[row(D_GROUP), row(D_GROUP), row(D), row(p.shape[-1])] + [full(w) for w in weights],
        out_specs=row(D),
        out_shape=jax.ShapeDtypeStruct((B, S, D), F32),
        compiler_params=pltpu.CompilerParams(dimension_semantics=("arbitrary", "arbitrary"),
                                             vmem_limit_bytes=VMEM_LIMIT),
        name="proj_out",
    )(sb_y, mla_y, x, p, *weights)


def _rotate_half_cols(w):
    half = w.shape[-1] // 2
    return jnp.concatenate([-w[..., half:], w[..., :half]], axis=-1)


def _head_lanes(nope, rope):
    K = nope.shape[0]
    pad = jnp.zeros((K, N_HEADS, LANES - QK_NOPE_DIM - QK_ROPE_DIM), nope.dtype)
    return jnp.concatenate([nope, rope, pad], axis=-1).reshape(K, N_HEADS * LANES)


def _layer(x, p, pos, norm_pre_g, w_in, q_norm_g, w_uq, kv_norm_g, w_ukv, sb_out_norm_g, mla_out_norm_g, w_out,
           norm_post_g, w_ple, ple_norm_g, w_ple_gate, b_ple_gate):
    S = x.shape[1]
    row = lambda a: a.reshape(1, -1).astype(F32)
    c = np.cumsum([0, D_GROUP, D_GROUP, D_GROUP, D_GROUP, Q_LORA_RANK, KV_LORA_RANK, QK_ROPE_DIM, D_GROUP])
    wq, wk, wv, wg, wcq, wckv, wkr, wmg = (w_in[:, c[n]:c[n + 1]] for n in range(8))

    def rope_lanes(w):
        return jnp.pad(w, ((0, 0), (QK_NOPE_DIM, LANES - QK_NOPE_DIM - QK_ROPE_DIM)))

    wkr2 = jnp.concatenate([rope_lanes(wkr), rope_lanes(_rotate_half_cols(wkr))], axis=-1)
    uq = w_uq.reshape(Q_LORA_RANK, N_HEADS, QK_NOPE_DIM + QK_ROPE_DIM)
    uq_nope, uq_rope = uq[..., :QK_NOPE_DIM], uq[..., QK_NOPE_DIM:]
    wuq2 = jnp.concatenate([_head_lanes(uq_nope, uq_rope),
                            _head_lanes(jnp.zeros_like(uq_nope), _rotate_half_cols(uq_rope))], axis=-1)
    ukv = w_ukv.reshape(KV_LORA_RANK, N_HEADS, QK_NOPE_DIM + HEAD_DIM)
    uk, uv = ukv[..., :QK_NOPE_DIM], ukv[..., QK_NOPE_DIM:]
    wukv2 = jnp.concatenate([_head_lanes(uk, jnp.zeros((KV_LORA_RANK, N_HEADS, QK_ROPE_DIM), uk.dtype)),
                             uv.reshape(KV_LORA_RANK, D_GROUP)], axis=-1)
    half = QK_ROPE_DIM // 2
    freq = (ROPE_THETA ** (-jnp.arange(half, dtype=F32) / half)).reshape(half, 1)
    bf = lambda a: a.astype(BF16)

    sbq, sbk, sbv, sbg, mq, mk, mv, mg, stats = _projections(
        x, pos.astype(F32)[:, None, :], freq, row(norm_pre_g), bf(wq), bf(wk), bf(wv), bf(wg), bf(wcq), bf(wckv),
        bf(wkr2), bf(wmg), row(q_norm_g), bf(wuq2), row(kv_norm_g), bf(wukv2))
    sb_k_spec = pl.BlockSpec((1, S, D_GROUP), lambda b, i: (b, 0, 0))
    mla_k_spec = pl.BlockSpec((1, N_HEADS, S, LANES), lambda b, i: (b, 0, 0, 0))
    sb_y = _attention(_sb_kernel, "sb_attn", SB_TQ, sbq, sbk, sbv, sbg, row(sb_out_norm_g), sb_k_spec, HEAD_DIM, 1)
    mla_y = _attention(_mla_kernel, "mla_attn", MLA_TQ, mq, mk, mv, mg, row(mla_out_norm_g), mla_k_spec,
                       HEAD_DIM + SUM_ROWS, 0, (stats,))
    return _output(sb_y, mla_y, x, p, bf(w_out[:D_GROUP]), bf(w_out[D_GROUP:]), row(norm_post_g), bf(w_ple),
                   row(ple_norm_g), bf(w_ple_gate), row(b_ple_gate))


def kernel(x, p, positions, norm_pre_g, w_in, q_norm_g, w_uq, kv_norm_g, w_ukv, sb_out_norm_g, mla_out_norm_g, w_out,
           norm_post_g, w_ple, ple_norm_g, w_ple_gate, b_ple_gate):
    for i in range(p.shape[0]):
        x = _layer(x, p[i], positions, norm_pre_g[i], w_in[i], q_norm_g[i], w_uq[i], kv_norm_g[i], w_ukv[i],
                   sb_out_norm_g[i], mla_out_norm_g[i], w_out[i], norm_post_g[i], w_ple[i], ple_norm_g[i],
                   w_ple_gate[i], b_ple_gate[i])
    return x
```

```python
import jax
import jax.numpy as jnp
import numpy as np
from jax import lax
from jax.experimental import pallas as pl
from jax.experimental.pallas import tpu as pltpu

HEAD_DIM = 64
N_HEADS = 8
N_PAIRS = N_HEADS // 2
LANES = 128
QK_NOPE_DIM = 64
QK_ROPE_DIM = 32
Q_LORA_RANK = 256
KV_LORA_RANK = 128
CHUNK = 64
ROPE_THETA = 10000.0
EPS = 1e-6
D_GROUP = N_HEADS * HEAD_DIM
LOG2E = 1.4426950408889634

PROJ_ROWS = 1024
SB_TK = 256
SB_TQ = 256
MLA_TK = 1024
MLA_TQ = 1024
DIAG_BANDS = 4
FAST_LOG2 = 60.0
NORM_SLACK = 1.02
SUM_ROWS = 16
LOOKAHEAD = 3
SKIP_LOG2 = 152.0
EXP2_CLAMP = 126.0
VMEM_LIMIT = 56 * 1024 * 1024

F32 = jnp.float32
BF16 = jnp.bfloat16


def _rms(x, g):
    return x * lax.rsqrt(jnp.mean(x * x, axis=-1, keepdims=True) + EPS) * g


def _dot(a, b):
    return jnp.dot(a, b, preferred_element_type=F32)


def _dot_nt(a, b):
    return lax.dot_general(a, b, (((1,), (1,)), ((), ())), preferred_element_type=F32)


def _sigmoid(x):
    return 1.0 / (1.0 + jnp.exp(-x))


def _proj_kernel(x_ref, pos_ref, freq_ref, gpre_ref, wq_ref, wk_ref, wv_ref, wg_ref, wcq_ref, wckv_ref,
                 wkr_ref, wmg_ref, gq_ref, wuq_ref, gkv_ref, wukv_ref,
                 sbq_ref, sbk_ref, sbv_ref, sbg_ref, mq_ref, mk_ref, mv_ref, mg_ref, stats_ref):
    h = _rms(x_ref[...], gpre_ref[...]).astype(BF16)
    n_stat = stats_ref.shape[1]

    def max_sq_norm(a):
        sq = jnp.sum(a * a, axis=-1, keepdims=True)
        rows = a.shape[0] // n_stat
        return [jnp.broadcast_to(jnp.max(sq[c * rows:(c + 1) * rows], axis=0, keepdims=True), (1, LANES))
                for c in range(n_stat)]

    def store_transposed(a, out_ref):
        tk = out_ref.shape[-1]
        for p in range(N_PAIRS):
            at = a[:, p * LANES:(p + 1) * LANES].T.astype(BF16)
            for c in range(out_ref.shape[2]):
                out_ref[0, 2 * p, c] = at[:HEAD_DIM, c * tk:(c + 1) * tk]
                out_ref[0, 2 * p + 1, c] = at[HEAD_DIM:, c * tk:(c + 1) * tk]

    sbq_ref[0] = (_dot(h, wq_ref[...]) * (HEAD_DIM ** -0.5 * LOG2E)).astype(BF16)
    sbk_ref[0] = _dot(h, wk_ref[...]).astype(BF16)
    store_transposed(_dot(h, wv_ref[...]), sbv_ref)
    sbg_ref[0] = _dot(h, wg_ref[...])
    mg_ref[0] = _dot(h, wmg_ref[...])

    ang_t = freq_ref[...] * pos_ref[0]
    n_freq = ang_t.shape[0]
    frow = lax.broadcasted_iota(jnp.int32, (n_freq, LANES), 0)
    rope_lane = lax.broadcasted_iota(jnp.int32, (n_freq, LANES), 1) - QK_NOPE_DIM
    spread = ((rope_lane == frow) | (rope_lane - n_freq == frow)).astype(BF16)

    def on_lanes(t):
        hi = t.astype(BF16)
        lo = (t - hi.astype(F32)).astype(BF16)
        tn = lambda a: lax.dot_general(a, spread, (((0,), (0,)), ((), ())), preferred_element_type=F32)
        return tn(hi) + tn(lo)

    nope_lanes = (lax.broadcasted_iota(jnp.int32, (1, LANES), 1) < QK_NOPE_DIM).astype(F32)
    cos, sin = on_lanes(jnp.cos(ang_t)) + nope_lanes, on_lanes(jnp.sin(ang_t))

    cq = _rms(_dot(h, wcq_ref[...]), gq_ref[...]).astype(BF16)
    q2 = _dot(cq, wuq_ref[...])
    scale = (QK_NOPE_DIM + QK_ROPE_DIM) ** -0.5 * LOG2E
    cos_q, sin_q = cos * scale, sin * scale
    half = N_HEADS * LANES
    q_norms, k_norms = [], []
    for hd in range(N_HEADS):
        qa = q2[:, hd * LANES:(hd + 1) * LANES]
        qb = q2[:, half + hd * LANES:half + (hd + 1) * LANES]
        qh = qa * cos_q + qb * sin_q
        q_norms.append(max_sq_norm(qh))
        mq_ref[0, :, hd * LANES:(hd + 1) * LANES] = qh.astype(BF16)

    ckv = _rms(_dot(h, wckv_ref[...]), gkv_ref[...]).astype(BF16)
    kv = _dot(ckv, wukv_ref[...])
    kr2 = _dot(h, wkr_ref[...])
    kr = kr2[:, :LANES] * cos + kr2[:, LANES:] * sin
    for hd in range(N_HEADS):
        kh = kv[:, hd * LANES:(hd + 1) * LANES] + kr
        k_norms.append(max_sq_norm(kh))
        mk_ref[0, hd] = kh.astype(BF16)
    store_transposed(kv[:, half:], mv_ref)
    for c in range(n_stat):
        stats_ref[0, c] = jnp.concatenate([n[c] for n in k_norms] + [n[c] for n in q_norms], axis=0)


def _projections(x, pos, freq, gpre, wq, wk, wv, wg, wcq, wckv, wkr, wmg, gq, wuq, gkv, wukv):
    B, S, D = x.shape
    tm = min(PROJ_ROWS, S)
    assert S % tm == 0
    nt = S // tm
    full = lambda a: pl.BlockSpec(a.shape, lambda b, t: (0,) * a.ndim)
    row = lambda w: pl.BlockSpec((1, tm, w), lambda b, t: (b, t, 0))
    head = pl.BlockSpec((1, N_HEADS, tm, LANES), lambda b, t: (b, 0, t, 0))

    def vt(tile):
        tk = min(tile, S)
        assert tm % tk == 0
        return (pl.BlockSpec((1, N_HEADS, tm // tk, HEAD_DIM, tk), lambda b, t: (b, 0, t, 0, 0)),
                jax.ShapeDtypeStruct((B, N_HEADS, S // tk, HEAD_DIM, tk), BF16))

    (sb_vt, sb_vt_shape), (mla_vt, mla_vt_shape) = vt(SB_TK), vt(MLA_TK)
    n_stat = tm // min(MLA_TK, S)
    stats = pl.BlockSpec((1, n_stat, 2 * N_HEADS, LANES), lambda b, t: (b, t, 0, 0))
    stats_shape = jax.ShapeDtypeStruct((B, nt * n_stat, 2 * N_HEADS, LANES), F32)
    weights = (freq, gpre, wq, wk, wv, wg, wcq, wckv, wkr, wmg, gq, wuq, gkv, wukv)
    return pl.pallas_call(
        _proj_kernel,
        grid=(B, nt),
        in_specs=[pl.BlockSpec((None, tm, D), lambda b, t: (b, t, 0)),
                  pl.BlockSpec((1, 1, tm), lambda b, t: (b, 0, t))] + [full(w) for w in weights],
        out_specs=[row(D_GROUP), row(D_GROUP), sb_vt, row(D_GROUP), row(N_HEADS * LANES), head, mla_vt, row(D_GROUP),
                   stats],
        out_shape=[jax.ShapeDtypeStruct((B, S, D_GROUP), BF16), jax.ShapeDtypeStruct((B, S, D_GROUP), BF16),
                   sb_vt_shape,
                   jax.ShapeDtypeStruct((B, S, D_GROUP), F32),
                   jax.ShapeDtypeStruct((B, S, N_HEADS * LANES), BF16),
                   jax.ShapeDtypeStruct((B, N_HEADS, S, LANES), BF16), mla_vt_shape,
                   jax.ShapeDtypeStruct((B, S, D_GROUP), F32), stats_shape],
        compiler_params=pltpu.CompilerParams(dimension_semantics=("arbitrary", "arbitrary"),
                                             vmem_limit_bytes=VMEM_LIMIT),
        name="proj_in",
    )(x, pos, *weights)


def _store_heads(o_t, g_ref, gate_ref, o_ref):
    for p in range(N_PAIRS):
        normed = [o * lax.rsqrt(jnp.mean(o * o, axis=0, keepdims=True) + EPS) for o in o_t[2 * p:2 * p + 2]]
        o_pair = jnp.concatenate(normed, axis=0).T
        cols = slice(p * LANES, (p + 1) * LANES)
        gate = gate_ref[0, :, cols]
        o_ref[0, :, cols] = (o_pair * g_ref[:, cols] * (gate * _sigmoid(gate))).astype(o_ref.dtype)


def _pipelined_heads(score, first, second, n_units=N_HEADS, stash_ref=None, stashed=False, score_next=None):
    s = [stash_ref[u] if stashed else score(u) for u in range(LOOKAHEAD)]
    mid = []
    for u in range(n_units):
        mid.append(first(u, s[u]))
        if u + LOOKAHEAD < n_units:
            s.append(score(u + LOOKAHEAD))
        elif stash_ref is not None:
            stash_ref[u + LOOKAHEAD - n_units] = score_next(u + LOOKAHEAD - n_units)
        if second is not None and u > 0:
            second(u - 1, s[u - 1], mid[u - 1])
    if second is not None:
        second(n_units - 1, s[-1], mid[-1])


def _sb_kernel(q_ref, k_ref, v_ref, gate_ref, g_ref, o_ref, acc_ref, carry_ref, stash_ref):
    tq = q_ref.shape[1]
    tk = v_ref.shape[-1]
    assert tq == tk
    i = pl.program_id(1)
    low_half = lax.broadcasted_iota(jnp.int32, (tq, LANES), 1) < HEAD_DIM
    qs = []
    for p in range(N_PAIRS):
        qp = q_ref[0, :, p * LANES:(p + 1) * LANES]
        qs += [jnp.where(low_half, qp, 0), jnp.where(low_half, 0, qp)]
    upper = (lax.broadcasted_iota(jnp.int32, (tk, tk), 1) >= lax.broadcasted_iota(jnp.int32, (tk, tk), 0)).astype(BF16)

    acc_ref[...] = jnp.zeros_like(acc_ref)
    carry_ref[...] = jnp.zeros_like(carry_ref)

    def raw_score(h, rows, q0, qn):
        pair = slice((h // 2) * LANES, (h // 2 + 1) * LANES)
        return _dot_nt(k_ref[0, rows, pair], qs[h][q0:q0 + qn])

    def step(blocks, j_next, stashed):
        def split(u):
            return blocks[u // N_HEADS], u % N_HEADS

        def score(u):
            (j, k0, kn, q0, qn, masked), h = split(u)
            z = raw_score(h, pl.ds(pl.multiple_of(j * tk + k0, kn), kn), q0, qn)
            if masked:
                past = (lax.broadcasted_iota(jnp.int32, (kn, qn), 0) + k0
                        < lax.broadcasted_iota(jnp.int32, (kn, qn), 1) + q0)
                z = jnp.where(past, z, -jnp.inf)
            return z

        def suffix(u, z):
            kn = z.shape[0]
            sp = jnp.maximum(jnp.log2(1.0 + jnp.exp2(jnp.minimum(z, EXP2_CLAMP))), z)
            return _dot(upper[:kn, :kn], sp.astype(BF16))

        def weigh(u, z, c):
            (j, k0, kn, q0, qn, _), h = split(u)
            carry = carry_ref[h:h + 1, q0:q0 + qn]
            w = jnp.exp2(z - c - carry)
            acc_ref[h, :, q0:q0 + qn] += _dot(v_ref[0, h, j, :, k0:k0 + kn], w.astype(BF16))
            carry_ref[h:h + 1, q0:q0 + qn] = carry + c[0:1, :]

        next_rows = pl.ds(pl.multiple_of(j_next * tk, tk), tk)
        _pipelined_heads(score, suffix, weigh, n_units=len(blocks) * N_HEADS, stash_ref=stash_ref, stashed=stashed,
                         score_next=lambda h: raw_score(h, next_rows, 0, tq))

    step([(i, 0, tk, 0, tq, True)], jnp.maximum(i - 1, 0), False)

    def body(state):
        n, _ = state
        step([(i - 1 - n, 0, tk, 0, tq, False)], jnp.maximum(i - 2 - n, 0), True)
        return n + 1, jnp.min(carry_ref[...])

    lax.while_loop(lambda st: (st[0] < i) & (st[1] < SKIP_LOG2), body, (jnp.int32(0), jnp.float32(0.0)))
    _store_heads([acc_ref[h] for h in range(N_HEADS)], g_ref, gate_ref, o_ref)


def _mla_kernel(q_ref, k_ref, v_ref, gate_ref, g_ref, stats_ref, o_ref, acc_ref, m_ref):
    tq = q_ref.shape[1]
    tk = v_ref.shape[-1]
    ratio = tq // tk
    assert tk % (tq // DIAG_BANDS) == 0
    i = pl.program_id(1)

    acc_ref[...] = jnp.zeros_like(acc_ref)

    def unit(blocks, u):
        return blocks[u // N_HEADS], u % N_HEADS

    def score(blocks, u):
        (j, k0, kn, q0, qn, _), h = unit(blocks, u)
        rows = pl.ds(pl.multiple_of(j * tk + k0, kn), kn)
        return _dot_nt(k_ref[0, h, rows, :], q_ref[0, q0:q0 + qn, h * LANES:(h + 1) * LANES])

    def visible(kn, qn, q0, key_off):
        key = lax.broadcasted_iota(jnp.int32, (kn, qn), 0) + key_off
        qry = lax.broadcasted_iota(jnp.int32, (kn, qn), 1) + q0
        return (key // CHUNK) <= (qry // CHUNK)

    def values_and_ones(h, j, k0, kn):
        return jnp.concatenate([v_ref[0, h, j, :, k0:k0 + kn], jnp.ones((SUM_ROWS, kn), BF16)], axis=0)

    def step(blocks):
        def update(u, s):
            (j, k0, kn, q0, qn, key_off), h = unit(blocks, u)
            if key_off is not None:
                s = jnp.where(visible(kn, qn, q0, key_off), s, -jnp.inf)
            m = m_ref[h:h + 1, q0:q0 + qn]
            m_new = jnp.maximum(m, jnp.max(s, axis=0, keepdims=True))
            alpha = jnp.exp2(m - m_new)
            p = jnp.exp2(s - m_new)
            m_ref[h:h + 1, q0:q0 + qn] = m_new
            acc_ref[h, :, q0:q0 + qn] = (alpha * acc_ref[h, :, q0:q0 + qn]
                                         + _dot(values_and_ones(h, j, k0, kn), p.astype(BF16)))

        _pipelined_heads(lambda u: score(blocks, u), update, None, n_units=len(blocks) * N_HEADS)

    def fast_step(blocks):
        def update(u, s):
            (j, k0, kn, q0, qn, key_off), h = unit(blocks, u)
            p = jnp.exp2(s - m_ref[h:h + 1, q0:q0 + qn])
            if key_off is not None:
                p = jnp.where(visible(kn, qn, q0, key_off), p, 0.0)
            acc_ref[h, :, q0:q0 + qn] += _dot(values_and_ones(h, j, k0, kn), p.astype(BF16))

        _pipelined_heads(lambda u: score(blocks, u), update, None, n_units=len(blocks) * N_HEADS)

    def sq_norms(first_tile, rows):
        sq = stats_ref[0, first_tile, rows, :]
        for c in range(1, ratio):
            sq = jnp.maximum(sq, stats_ref[0, first_tile + c, rows, :])
        return sq

    q_sq = sq_norms(i * ratio, slice(N_HEADS, 2 * N_HEADS))

    band = tq // DIAG_BANDS
    diagonal = [(i * ratio + (b * band) // tk, (b * band) % tk, band, b * band, tq - b * band, b * band)
                for b in range(DIAG_BANDS)]
    bound = jnp.sqrt(q_sq * sq_norms(i * ratio, slice(0, N_HEADS))) * NORM_SLACK
    m_ref[...] = jnp.tile(bound - FAST_LOG2, (1, tq // LANES))
    fast_step(diagonal)
    denom = acc_ref[:, HEAD_DIM, :]

    def renormalise():
        for h in range(N_HEADS):
            acc_ref[h] = acc_ref[h] / denom[h:h + 1, :]
        m_ref[...] = m_ref[...] + jnp.log2(denom)

    def redo():
        acc_ref[...] = jnp.zeros_like(acc_ref)
        m_ref[...] = jnp.full(m_ref.shape, -jnp.inf, F32)
        step(diagonal)

    lax.cond(jnp.min(denom) >= 2.0 ** -FAST_LOG2, renormalise, redo)

    def body(n, _):
        tile = [(n, 0, tk, 0, tq, None)]
        excess = jnp.max(jnp.sqrt(q_sq * stats_ref[0, n, :N_HEADS, :]) * NORM_SLACK
                         - jnp.min(m_ref[...], axis=1, keepdims=True))
        lax.cond(excess <= FAST_LOG2, lambda: fast_step(tile), lambda: step(tile))
        return 0

    lax.fori_loop(0, i * ratio, body, 0)
    _store_heads([acc_ref[h, :HEAD_DIM] / acc_ref[h, HEAD_DIM:HEAD_DIM + 1] for h in range(N_HEADS)],
                 g_ref, gate_ref, o_ref)


def _attention(body, name, tq, q, k, v, gate, g, k_spec, acc_rows, stash, per_batch=()):
    B, S, _ = gate.shape
    tq = min(tq, S)
    assert S % tq == 0 and tq % v.shape[-1] == 0
    return pl.pallas_call(
        body,
        grid=(B, S // tq),
        in_specs=[pl.BlockSpec((1, tq, q.shape[-1]), lambda b, i: (b, i, 0)),
                  k_spec,
                  pl.BlockSpec((1,) + v.shape[1:], lambda b, i: (b, 0, 0, 0, 0)),
                  pl.BlockSpec((1, tq, D_GROUP), lambda b, i: (b, i, 0)),
                  pl.BlockSpec((1, D_GROUP), lambda b, i: (0, 0))]
        + [pl.BlockSpec((1,) + a.shape[1:], lambda b, i: (b, 0, 0, 0)) for a in per_batch],
        out_specs=pl.BlockSpec((1, tq, D_GROUP), lambda b, i: (b, i, 0)),
        out_shape=jax.ShapeDtypeStruct((B, S, D_GROUP), BF16),
        scratch_shapes=[pltpu.VMEM((N_HEADS, acc_rows, tq), F32), pltpu.VMEM((N_HEADS, tq), F32)]
        + [pltpu.VMEM((LOOKAHEAD, v.shape[-1], tq), F32)] * stash,
        compiler_params=pltpu.CompilerParams(dimension_semantics=("arbitrary", "arbitrary"),
                                             vmem_limit_bytes=VMEM_LIMIT),
        name=name,
    )(q, k, v, gate, g, *per_batch)


def _out_kernel(sb_ref, mla_ref, x_ref, p_ref, wo_ref, gpost_ref, wple_ref, gple_ref, wpg_ref, bpg_ref,
                o_ref):
    y = _dot(jnp.concatenate([sb_ref[...], mla_ref[...]], axis=-1), wo_ref[...])
    x1 = x_ref[...] + _rms(y, gpost_ref[...])
    ple = _rms(_dot(p_ref[...].astype(BF16), wple_ref[...]), gple_ref[...])
    gate = _sigmoid(_dot(x1.astype(BF16), wpg_ref[...]) + bpg_ref[...])
    o_ref[...] = x1 + ple * gate


def _output(sb_y, mla_y, x, p, wo, gpost, wple, gple, wpg, bpg):
    B, S, D = x.shape
    tm = min(PROJ_ROWS, S)
    full = lambda a: pl.BlockSpec(a.shape, lambda b, t: (0,) * a.ndim)
    row = lambda w: pl.BlockSpec((None, tm, w), lambda b, t: (b, t, 0))
    weights = (wo, gpost, wple, gple, wpg, bpg)
    return pl.pallas_call(
        _out_kernel,
        grid=(B, S // tm),
        in_specs=[row(D_GROUP), row(D_GROUP), row(D), row(p.shape[-1])] + [full(w) for w in weights],
        out_specs=row(D),
        out_shape=jax.ShapeDtypeStruct((B, S, D), F32),
        compiler_params=pltpu.CompilerParams(dimension_semantics=("arbitrary", "arbitrary"),
                                             vmem_limit_bytes=VMEM_LIMIT),
        name="proj_out",
    )(sb_y, mla_y, x, p, *weights)


def _rotate_half_cols(w):
    half = w.shape[-1] // 2
    return jnp.concatenate([-w[..., half:], w[..., :half]], axis=-1)


def _head_lanes(nope, rope):
    K = nope.shape[0]
    pad = jnp.zeros((K, N_HEADS, LANES - QK_NOPE_DIM - QK_ROPE_DIM), nope.dtype)
    return jnp.concatenate([nope, rope, pad], axis=-1).reshape(K, N_HEADS * LANES)


def _layer(x, p, pos, norm_pre_g, w_in, q_norm_g, w_uq, kv_norm_g, w_ukv, sb_out_norm_g, mla_out_norm_g, w_out,
           norm_post_g, w_ple, ple_norm_g, w_ple_gate, b_ple_gate):
    S = x.shape[1]
    row = lambda a: a.reshape(1, -1).astype(F32)
    c = np.cumsum([0, D_GROUP, D_GROUP, D_GROUP, D_GROUP, Q_LORA_RANK, KV_LORA_RANK, QK_ROPE_DIM, D_GROUP])
    wq, wk, wv, wg, wcq, wckv, wkr, wmg = (w_in[:, c[n]:c[n + 1]] for n in range(8))

    def rope_lanes(w):
        return jnp.pad(w, ((0, 0), (QK_NOPE_DIM, LANES - QK_NOPE_DIM - QK_ROPE_DIM)))

    wkr2 = jnp.concatenate([rope_lanes(wkr), rope_lanes(_rotate_half_cols(wkr))], axis=-1)
    uq = w_uq.reshape(Q_LORA_RANK, N_HEADS, QK_NOPE_DIM + QK_ROPE_DIM)
    uq_nope, uq_rope = uq[..., :QK_NOPE_DIM], uq[..., QK_NOPE_DIM:]
    wuq2 = jnp.concatenate([_head_lanes(uq_nope, uq_rope),
                            _head_lanes(jnp.zeros_like(uq_nope), _rotate_half_cols(uq_rope))], axis=-1)
    ukv = w_ukv.reshape(KV_LORA_RANK, N_HEADS, QK_NOPE_DIM + HEAD_DIM)
    uk, uv = ukv[..., :QK_NOPE_DIM], ukv[..., QK_NOPE_DIM:]
    wukv2 = jnp.concatenate([_head_lanes(uk, jnp.zeros((KV_LORA_RANK, N_HEADS, QK_ROPE_DIM), uk.dtype)),
                             uv.reshape(KV_LORA_RANK, D_GROUP)], axis=-1)
    half = QK_ROPE_DIM // 2
    freq = (ROPE_THETA ** (-jnp.arange(half, dtype=F32) / half)).reshape(half, 1)
    bf = lambda a: a.astype(BF16)

    sbq, sbk, sbv, sbg, mq, mk, mv, mg, stats = _projections(
        x, pos.astype(F32)[:, None, :], freq, row(norm_pre_g), bf(wq), bf(wk), bf(wv), bf(wg), bf(wcq), bf(wckv),
        bf(wkr2), bf(wmg), row(q_norm_g), bf(wuq2), row(kv_norm_g), bf(wukv2))
    sb_k_spec = pl.BlockSpec((1, S, D_GROUP), lambda b, i: (b, 0, 0))
    mla_k_spec = pl.BlockSpec((1, N_HEADS, S, LANES), lambda b, i: (b, 0, 0, 0))
    sb_y = _attention(_sb_kernel, "sb_attn", SB_TQ, sbq, sbk, sbv, sbg, row(sb_out_norm_g), sb_k_spec, HEAD_DIM, 1)
    mla_y = _attention(_mla_kernel, "mla_attn", MLA_TQ, mq, mk, mv, mg, row(mla_out_norm_g), mla_k_spec,
                       HEAD_DIM + SUM_ROWS, 0, (stats,))
    return _output(sb_y, mla_y, x, p, bf(w_out), row(norm_post_g), bf(w_ple),
                   row(ple_norm_g), bf(w_ple_gate), row(b_ple_gate))


def kernel(x, p, positions, norm_pre_g, w_in, q_norm_g, w_uq, kv_norm_g, w_ukv, sb_out_norm_g, mla_out_norm_g, w_out,
           norm_post_g, w_ple, ple_norm_g, w_ple_gate, b_ple_gate):
    for i in range(p.shape[0]):
        x = _layer(x, p[i], positions, norm_pre_g[i], w_in[i], q_norm_g[i], w_uq[i], kv_norm_g[i], w_ukv[i],
                   sb_out_norm_g[i], mla_out_norm_g[i], w_out[i], norm_post_g[i], w_ple[i], ple_norm_g[i],
                   w_ple_gate[i], b_ple_gate[i])
    return x
```
